```python
import math
import jax
import jax.numpy as jnp
from jax import lax
import numpy as np

D_MODEL = 1024
BATCH = 8
SEQ = 4096
DEPTH = 2

GRID_W = 64
CTX_LEN = 256
GDN_HEAD_DIM = 128
GDN_HEADS = D_MODEL // (2 * GDN_HEAD_DIM)
GDN_WIDTH = GDN_HEADS * GDN_HEAD_DIM
CONV_W = 5
CHUNK = 64
ATT_HEAD_DIM = 128
ATT_Q_HEADS = D_MODEL // (2 * ATT_HEAD_DIM)
ATT_KV_HEADS = ATT_Q_HEADS // 2
ATT_WIDTH = ATT_Q_HEADS * ATT_HEAD_DIM
ATT_KV_WIDTH = ATT_KV_HEADS * ATT_HEAD_DIM
ATT_GROUP = ATT_Q_HEADS // ATT_KV_HEADS
ROPE_AXIS_DIM = ATT_HEAD_DIM // 2
ROPE_THETA = 10000.0
Q_BLOCK = 128
D_MIX = GDN_WIDTH + ATT_WIDTH
OFF_QKV = 0
OFF_Z = 3 * GDN_WIDTH
OFF_BA = 4 * GDN_WIDTH
OFF_ATT = OFF_BA + 4 * GDN_HEADS
N_IN = OFF_ATT + ATT_WIDTH + 2 * ATT_KV_WIDTH
N_EXPERTS = 256
TOP_K = 8
N_GROUPS = 8
TOPK_GROUPS = 4
EXPERT_DIM = 256
SHARED_DIM = 256
ROUTED_SCALE = 2.5
DISPATCH_BLOCK = 128
LN_EPS = 1e-5
RMS_EPS = 1e-6

kernel_name = 'hybrid_gdn_gqa_moe_diffusion_block'


def _standardize(x, eps):
    xf = x.astype(jnp.float32)
    mu = jnp.mean(xf, -1, keepdims=True)
    var = jnp.mean(jnp.square(xf - mu), -1, keepdims=True)
    return (xf - mu) * lax.rsqrt(var + eps)


def layer_norm(x, w, b):
    return (_standardize(x, LN_EPS) * w.astype(jnp.float32) + b.astype(jnp.float32)).astype(x.dtype)


def rms_norm(x, w):
    xf = x.astype(jnp.float32)
    y = xf * lax.rsqrt(jnp.mean(xf * xf, -1, keepdims=True) + RMS_EPS) * w.astype(jnp.float32)
    return y.astype(x.dtype)


def l2_normalize(x):
    xf = x.astype(jnp.float32)
    return xf * lax.rsqrt(jnp.sum(xf * xf, -1, keepdims=True) + RMS_EPS)


def axial_rope_tables(rows):
    row = jnp.repeat(jnp.arange(rows, dtype=jnp.float32), GRID_W)
    col = jnp.tile(jnp.arange(GRID_W, dtype=jnp.float32), rows)
    inv_freq = ROPE_THETA ** (-jnp.arange(0, ROPE_AXIS_DIM, 2, dtype=jnp.float32) / ROPE_AXIS_DIM)
    ang = jnp.stack([row[:, None] * inv_freq, col[:, None] * inv_freq], axis=1)
    return jnp.cos(ang), jnp.sin(ang)


def apply_axial_rope(x, cos, sin):
    B, L, H, hd = x.shape
    F = ROPE_AXIS_DIM // 2
    xf = x.astype(jnp.float32).reshape(B, L, H, 2, 2, F)
    x1, x2 = xf[..., 0, :], xf[..., 1, :]
    c, s = cos[None, :, None], sin[None, :, None]
    out = jnp.stack([x1 * c - x2 * s, x2 * c + x1 * s], axis=-2)
    return out.reshape(B, L, H, hd).astype(x.dtype)


def centred_depthwise_conv(x, w):
    return lax.conv_general_dilated(x, w[:, None, :].astype(x.dtype), window_strides=(1,),
                                    padding=[(CONV_W // 2, CONV_W // 2)],
                                    dimension_numbers=('NWC', 'WIO', 'NWC'),
                                    feature_group_count=x.shape[-1])


def chunk_gated_delta(q, k, v, g, beta, s0):
    f32 = jnp.float32
    B, H, L, dk = q.shape
    dv = v.shape[-1]
    n = L // CHUNK
    q, k, v = (t.astype(f32).reshape(B, H, n, CHUNK, -1) for t in (q, k, v))
    g = g.astype(f32).reshape(B, H, n, CHUNK)
    beta = beta.astype(f32).reshape(B, H, n, CHUNK)
    G = jnp.cumsum(g, axis=-1)
    idx = jnp.arange(CHUNK)
    incl = idx[:, None] >= idx[None, :]
    strict = idx[:, None] > idx[None, :]
    diff = G[..., :, None] - G[..., None, :]
    decay = jnp.where(incl, jnp.exp(jnp.where(incl, diff, 0.0)), 0.0)
    kk = jnp.einsum('bhnid,bhnjd->bhnij', k, k)
    a_mat = jnp.where(strict, beta[..., :, None] * kk * decay, 0.0)
    m = a_mat + jnp.eye(CHUNK, dtype=f32)
    rhs = jnp.concatenate([v * beta[..., None], k * (beta * jnp.exp(G))[..., None]], axis=-1)
    sol = lax.linalg.triangular_solve(m, rhs, left_side=True, lower=True, unit_diagonal=True)
    u, w = sol[..., :dv], sol[..., dv:]
    qk = jnp.einsum('bhnid,bhnjd->bhnij', q, k) * decay
    q_dec = q * jnp.exp(G)[..., None]
    k_dec = k * jnp.exp(G[..., -1:] - G)[..., None]
    c_dec = jnp.exp(G[..., -1])

    def step(s, xs):
        u_c, w_c, qk_c, qd_c, kd_c, cd_c = xs
        v_new = u_c - jnp.einsum('bhcd,bhde->bhce', w_c, s)
        o_c = jnp.einsum('bhcd,bhde->bhce', qd_c, s) + jnp.einsum('bhcj,bhje->bhce', qk_c, v_new)
        s = s * cd_c[..., None, None] + jnp.einsum('bhcd,bhce->bhde', kd_c, v_new)
        return s, o_c

    xs = tuple(jnp.moveaxis(t, 2, 0) for t in (u, w, qk, q_dec, k_dec, c_dec))
    s_final, o = lax.scan(step, s0.astype(f32), xs)
    return jnp.moveaxis(o, 0, 2).reshape(B, H, L, dv), s_final


def bidirectional_delta(q, k, v, g, beta, s0_fwd, s0_bwd):
    o_f, s_f = chunk_gated_delta(q, k, v, g[0], beta[0], s0_fwd)
    fl = lambda t: jnp.flip(t, axis=2)
    o_b, s_b = chunk_gated_delta(fl(q), fl(k), fl(v), fl(g[1]), fl(beta[1]), s0_bwd)
    return o_f + fl(o_b), s_f, s_b


def gdn_inputs(p, conv_w, a_log, dt_bias):
    f32 = jnp.float32
    B, L, _ = p.shape
    qkv = jax.nn.silu(centred_depthwise_conv(p[..., OFF_QKV:OFF_Z], conv_w))
    heads = lambda t: t.reshape(B, L, GDN_HEADS, GDN_HEAD_DIM).transpose(0, 2, 1, 3)
    q, k, v = (heads(qkv[..., i * GDN_WIDTH:(i + 1) * GDN_WIDTH]) for i in range(3))
    q = l2_normalize(q) * GDN_HEAD_DIM ** -0.5
    k = l2_normalize(k)
    ba = p[..., OFF_BA:OFF_ATT].astype(f32).reshape(B, L, 2, 2, GDN_HEADS)
    beta = jax.nn.sigmoid(ba[:, :, 0]).transpose(2, 0, 3, 1)
    g = (-jnp.exp(a_log.astype(f32)) * jax.nn.softplus(ba[:, :, 1] + dt_bias.astype(f32))).transpose(2, 0, 3, 1)
    return q, k, v, g, beta


def gdn_output(o, z, norm_w):
    B, H, L, dv = o.shape
    y = rms_norm(o.transpose(0, 2, 1, 3), norm_w) * jax.nn.silu(z.astype(jnp.float32).reshape(B, L, H, dv))
    return y.reshape(B, L, GDN_WIDTH).astype(z.dtype)


def attn_inputs(p, q_norm_w, k_norm_w):
    B, L, _ = p.shape
    a = p[..., OFF_ATT:]
    q = rms_norm(a[..., :ATT_WIDTH].reshape(B, L, ATT_Q_HEADS, ATT_HEAD_DIM), q_norm_w)
    k = rms_norm(a[..., ATT_WIDTH:ATT_WIDTH + ATT_KV_WIDTH].reshape(B, L, ATT_KV_HEADS, ATT_HEAD_DIM), k_norm_w)
    v = a[..., ATT_WIDTH + ATT_KV_WIDTH:].reshape(B, L, ATT_KV_HEADS, ATT_HEAD_DIM)
    return q, k, v


def attend(q, k, v):
    s = jnp.einsum('bqhgd,bkhd->bhgqk', q, k, preferred_element_type=jnp.float32) * ATT_HEAD_DIM ** -0.5
    p = jax.nn.softmax(s, axis=-1).astype(v.dtype)
    return jnp.einsum('bhgqk,bkhd->bqhgd', p, v)


def latent_attention(q, k, v, k_ctx, v_ctx):
    B, L, _, _ = q.shape
    kk = jnp.concatenate([k, k_ctx], axis=1)
    vv = jnp.concatenate([v, v_ctx], axis=1)
    nb = L // Q_BLOCK
    qb = q.reshape(B, nb, Q_BLOCK, ATT_KV_HEADS, ATT_GROUP, ATT_HEAD_DIM).swapaxes(0, 1)
    o = lax.map(lambda qblk: attend(qblk, kk, vv), qb)
    return o.swapaxes(0, 1).reshape(B, L, ATT_WIDTH)


def hybrid_mixer(h, hc, w_in, conv_w, a_log, dt_bias, gdn_norm_w, q_norm_w, k_norm_w, w_out, cos, sin, ctx_out):
    B, Lc, _ = hc.shape
    p = h @ w_in
    pc = hc @ w_in
    q, k, v, g, beta = gdn_inputs(p, conv_w, a_log, dt_bias)
    qc, kc, vc, gc, betac = gdn_inputs(pc, conv_w, a_log, dt_bias)
    s0 = jnp.zeros((B, GDN_HEADS, GDN_HEAD_DIM, GDN_HEAD_DIM), jnp.float32)
    oc, s_f, s_b = bidirectional_delta(qc, kc, vc, gc, betac, s0, s0)
    o, _, _ = bidirectional_delta(q, k, v, g, beta, s_f, s_b)
    gdn = gdn_output(o, p[..., OFF_Z:OFF_BA], gdn_norm_w)
    qa, ka, va = attn_inputs(p, q_norm_w, k_norm_w)
    qa, ka = apply_axial_rope(qa, cos, sin), apply_axial_rope(ka, cos, sin)
    qac, kac, vac = attn_inputs(pc, q_norm_w, k_norm_w)
    att = latent_attention(qa, ka, va, kac, vac)
    y = jnp.concatenate([gdn, att], axis=-1) @ w_out
    if not ctx_out:
        return y, None
    gdn_c = gdn_output(oc, pc[..., OFF_Z:OFF_BA], gdn_norm_w)
    att_c = attend(qac.reshape(B, Lc, ATT_KV_HEADS, ATT_GROUP, ATT_HEAD_DIM), kac, vac).reshape(B, Lc, ATT_WIDTH)
    yc = jnp.concatenate([gdn_c, att_c], axis=-1) @ w_out
    return y, yc


def moe_ffn(t, router_w, router_bias, w_gate, w_up, w_down, sh_gate, sh_up, sh_down):
    f32 = jnp.float32
    T, D = t.shape
    s = jax.nn.sigmoid(jnp.matmul(t, router_w, preferred_element_type=f32))
    sel = s + router_bias.astype(f32)
    per_group = N_EXPERTS // N_GROUPS
    grp_score = jnp.sum(lax.top_k(sel.reshape(T, N_GROUPS, per_group), 2)[0], axis=-1)
    _, top_groups = lax.top_k(grp_score, TOPK_GROUPS)
    group_mask = jnp.any(top_groups[:, :, None] == jnp.arange(N_GROUPS)[None, None, :], axis=1)
    expert_mask = jnp.repeat(group_mask, per_group, axis=1)
    _, eidx = lax.top_k(jnp.where(expert_mask, sel, -jnp.inf), TOP_K)
    wts = jnp.take_along_axis(s, eidx, axis=1)
    wts = wts / jnp.sum(wts, -1, keepdims=True) * ROUTED_SCALE
    A = T * TOP_K
    flat_e = eidx.reshape(A)
    flat_tok = jnp.repeat(jnp.arange(T, dtype=jnp.int32), TOP_K)
    flat_w = wts.reshape(A)
    order = jnp.argsort(flat_e)
    se = flat_e[order]
    counts = jnp.bincount(flat_e, length=N_EXPERTS)
    starts = jnp.cumsum(counts) - counts
    padded = (counts + DISPATCH_BLOCK - 1) // DISPATCH_BLOCK * DISPATCH_BLOCK
    pad_ends = jnp.cumsum(padded)
    pad_starts = pad_ends - padded
    dest = pad_starts[se] + (jnp.arange(A) - starts[se])
    n_blocks = -(-A // DISPATCH_BLOCK) + N_EXPERTS
    P = n_blocks * DISPATCH_BLOCK
    buf_tok = jnp.full((P,), T, jnp.int32).at[dest].set(flat_tok[order])
    buf_w = jnp.zeros((P,), f32).at[dest].set(flat_w[order])
    blk_e = jnp.minimum(jnp.searchsorted(pad_ends, jnp.arange(n_blocks) * DISPATCH_BLOCK, side='right'), N_EXPERTS - 1)
    t_pad = jnp.concatenate([t, jnp.zeros((1, D), t.dtype)], axis=0)

    def run_block(args):
        rows, wt, e = args
        xb = t_pad[rows]
        hb = jax.nn.silu(xb @ w_gate[e]) * (xb @ w_up[e])
        return (hb @ w_down[e]) * wt[:, None].astype(xb.dtype)

    out = lax.map(run_block, (buf_tok.reshape(n_blocks, DISPATCH_BLOCK), buf_w.reshape(n_blocks, DISPATCH_BLOCK), blk_e))
    routed = jnp.zeros((T + 1, D), t.dtype).at[buf_tok].add(out.reshape(P, D))[:T]
    shared = (jax.nn.silu(t @ sh_gate) * (t @ sh_up)) @ sh_down
    return routed + shared


def setup_inputs(seed: int = 0) -> dict:
    key = jax.random.key(seed)
    ks = jax.random.split(key, 26)
    f32 = jnp.float32
    nrm = lambda k, shape, scale: jax.random.normal(k, shape, f32) * scale
    out_scale = (8.0 * DEPTH) ** -0.25
    D = D_MODEL
    dt = jnp.exp(jax.random.uniform(ks[9], (DEPTH, 2, GDN_HEADS), f32, math.log(1e-3), math.log(1e-1)))
    return {
        'x': nrm(ks[0], (BATCH, SEQ, D), 1.0),
        'c': nrm(ks[1], (BATCH, D), 1.0),
        'ctx': nrm(ks[2], (BATCH, CTX_LEN, D), 1.0),
        'c_ctx': nrm(ks[3], (D,), 1.0),
        'ada_w': nrm(ks[4], (DEPTH, D, 6 * D), 0.5 * D ** -0.5),
        'ada_b': nrm(ks[5], (DEPTH, 6 * D), 0.02),
        'w_in': nrm(ks[6], (DEPTH, D, N_IN), D ** -0.5),
        'conv_w': nrm(ks[7], (DEPTH, CONV_W, 3 * GDN_WIDTH), CONV_W ** -0.5),
        'gdn_a_log': jnp.log(jax.random.uniform(ks[8], (DEPTH, 2, GDN_HEADS), f32, 1.0, 16.0)),
        'gdn_dt_bias': dt + jnp.log(-jnp.expm1(-dt)),
        'gdn_norm_w': 1.0 + nrm(ks[10], (DEPTH, GDN_HEAD_DIM), 0.1),
        'q_norm_w': 1.0 + nrm(ks[11], (DEPTH, ATT_HEAD_DIM), 0.1),
        'k_norm_w': 1.0 + nrm(ks[12], (DEPTH, ATT_HEAD_DIM), 0.1),
        'w_out': nrm(ks[13], (DEPTH, D_MIX, D), out_scale * D_MIX ** -0.5),
        'ln1_w': 1.0 + nrm(ks[14], (DEPTH, D), 0.1),
        'ln1_b': nrm(ks[15], (DEPTH, D), 0.02),
        'router_w': nrm(ks[16], (DEPTH, D, N_EXPERTS), D ** -0.5),
        'router_bias': nrm(ks[17], (DEPTH, N_EXPERTS), 0.01),
        'exp_w_gate': nrm(ks[18], (DEPTH, N_EXPERTS, D, EXPERT_DIM), D ** -0.5),
        'exp_w_up': nrm(ks[19], (DEPTH, N_EXPERTS, D, EXPERT_DIM), D ** -0.5),
        'exp_w_down': nrm(ks[20], (DEPTH, N_EXPERTS, EXPERT_DIM, D), out_scale * EXPERT_DIM ** -0.5),
        'sh_w_gate': nrm(ks[21], (DEPTH, D, SHARED_DIM), D ** -0.5),
        'sh_w_up': nrm(ks[22], (DEPTH, D, SHARED_DIM), D ** -0.5),
        'sh_w_down': nrm(ks[23], (DEPTH, SHARED_DIM, D), out_scale * SHARED_DIM ** -0.5),
        'ln2_w': 1.0 + nrm(ks[24], (DEPTH, D), 0.1),
        'ln2_b': nrm(ks[25], (DEPTH, D), 0.02),
    }


def reference(x, c, ctx, c_ctx, ada_w, ada_b, w_in, conv_w, gdn_a_log, gdn_dt_bias, gdn_norm_w, q_norm_w, k_norm_w,
              w_out, ln1_w, ln1_b, router_w, router_bias, exp_w_gate, exp_w_up, exp_w_down, sh_w_gate, sh_w_up,
              sh_w_down, ln2_w, ln2_b):
    B, L, D = x.shape
    Lc = ctx.shape[1]
    rows = L // GRID_W
    cos, sin = axial_rope_tables(rows)
    alpha = (2.0 * DEPTH) ** 0.25
    x = _standardize(x, LN_EPS).astype(x.dtype)
    xc = _standardize(ctx, LN_EPS).astype(ctx.dtype)
    for l in range(DEPTH):
        last = l == DEPTH - 1
        mod = jax.nn.silu(c) @ ada_w[l] + ada_b[l]
        mod_c = jax.nn.silu(c_ctx) @ ada_w[l] + ada_b[l]
        sh1, sc1, g1, sh2, sc2, g2 = jnp.split(mod[:, None, :], 6, axis=-1)
        sh1c, sc1c, g1c, sh2c, sc2c, g2c = jnp.split(mod_c[None, None, :], 6, axis=-1)
        y, yc = hybrid_mixer(x * (1.0 + sc1) + sh1, xc * (1.0 + sc1c) + sh1c, w_in[l], conv_w[l], gdn_a_log[l],
                             gdn_dt_bias[l], gdn_norm_w[l], q_norm_w[l], k_norm_w[l], w_out[l], cos, sin,
                             ctx_out=not last)
        x = layer_norm(alpha * x + g1 * y, ln1_w[l], ln1_b[l])
        h = x * (1.0 + sc2) + sh2
        moe_args = (router_w[l], router_bias[l], exp_w_gate[l], exp_w_up[l], exp_w_down[l],
                    sh_w_gate[l], sh_w_up[l], sh_w_down[l])
        if last:
            ff = moe_ffn(h.reshape(B * L, D), *moe_args).reshape(B, L, D)
        else:
            xc = layer_norm(alpha * xc + g1c * yc, ln1_w[l], ln1_b[l])
            hc = xc * (1.0 + sc2c) + sh2c
            ff_all = moe_ffn(jnp.concatenate([h.reshape(B * L, D), hc.reshape(B * Lc, D)], axis=0), *moe_args)
            ff = ff_all[:B * L].reshape(B, L, D)
            xc = layer_norm(alpha * xc + g2c * ff_all[B * L:].reshape(B, Lc, D), ln2_w[l], ln2_b[l])
        x = layer_norm(alpha * x + g2 * ff, ln2_w[l], ln2_b[l])
    return x
```

```python
import functools

import jax
import jax.numpy as jnp
from jax import lax
from jax.experimental import pallas as pl
from jax.experimental.pallas import tpu as pltpu

F32 = jnp.float32
BF16 = jnp.bfloat16

GRID_W = 64
GDN_HEAD_DIM = 128
GDN_HEADS = 4
GDN_WIDTH = GDN_HEADS * GDN_HEAD_DIM
CONV_W = 5
CHUNK = 64
ATT_HEAD_DIM = 128
ATT_Q_HEADS = 4
ATT_KV_HEADS = 2
ATT_GROUP = ATT_Q_HEADS // ATT_KV_HEADS
ATT_WIDTH = ATT_Q_HEADS * ATT_HEAD_DIM
ATT_KV_WIDTH = ATT_KV_HEADS * ATT_HEAD_DIM
ROPE_AXIS_DIM = ATT_HEAD_DIM // 2
ROPE_THETA = 10000.0
OFF_Z = 3 * GDN_WIDTH
OFF_BA = 4 * GDN_WIDTH
OFF_ATT = OFF_BA + 4 * GDN_HEADS
N_EXPERTS = 256
TOP_K = 8
N_GROUPS = 8
TOPK_GROUPS = 4
EXPERT_DIM = 256
ROUTED_SCALE = 2.5
LN_EPS = 1e-5
RMS_EPS = 1e-6

V7X_LANES = 128
V7X_VMEM_BYTES = 64 * 1024 * 1024
VMEM_CAP = V7X_VMEM_BYTES - 8 * 1024 * 1024

EXPERT_ROWS = 256
ATT_Q_TILE = 256
MM_ROWS = 1024


def _vmem_limit(estimate_bytes):
    return int(min(VMEM_CAP, max(16 * 1024 * 1024, 2 * estimate_bytes)))


def _mm_body(a_ref, w_ref, o_ref):
    o_ref[...] = jnp.dot(a_ref[...].astype(BF16), w_ref[...].astype(BF16),
                         preferred_element_type=F32).astype(o_ref.dtype)


def matmul(a, w, *, tm, tn, out_dtype=F32):
    M, K = a.shape
    N = w.shape[1]
    assert M % tm == 0 and N % tn == 0, (M, N, tm, tn)
    est = 2 * (tm * K * a.dtype.itemsize + K * tn * w.dtype.itemsize + tm * tn * 4) + tm * K * 2 + K * tn * 2
    return pl.pallas_call(
        _mm_body,
        grid=(M // tm, N // tn),
        in_specs=[pl.BlockSpec((tm, K), lambda i, j: (i, 0)),
                  pl.BlockSpec((K, tn), lambda i, j: (0, j))],
        out_specs=pl.BlockSpec((tm, tn), lambda i, j: (i, j)),
        out_shape=jax.ShapeDtypeStruct((M, N), out_dtype),
        compiler_params=pltpu.CompilerParams(dimension_semantics=("parallel", "parallel"),
                                             vmem_limit_bytes=_vmem_limit(est)),
        name="proj_matmul",
    )(a, w)


def _unit_lower_inverse(a_strict):
    n = a_strict.shape[0]
    ii = lax.broadcasted_iota(jnp.int32, (n, n), 0)
    jj = lax.broadcasted_iota(jnp.int32, (n, n), 1)
    hi = functools.partial(jnp.dot, precision=lax.Precision.HIGHEST, preferred_element_type=F32)
    bp = -a_strict
    prod = jnp.where(ii == jj, 1.0, 0.0) + bp
    steps = n.bit_length() - 2
    for _ in range(steps):
        bp = hi(bp, bp)
        prod = prod + hi(prod, bp)
    return prod


def _gdn_body(q_ref, k_ref, v_ref, gcol_ref, grow_ref, o_ref,
              u_scr, wq_scr, kd_scr, qk_scr, cd_scr, *, n_chunks, n_ctx_chunks, chunks_per_iter):
    C = CHUNK
    dn_t = (((1,), (1,)), ((), ()))
    ii = lax.broadcasted_iota(jnp.int32, (C, C), 0)
    jj = lax.broadcasted_iota(jnp.int32, (C, C), 1)
    incl = (ii >= jj, ii <= jj)
    strict = (ii > jj, ii < jj)

    def precompute(c):
        r0 = pl.multiple_of(c * C, C)
        q = q_ref[0, 0, pl.ds(r0, C), :]
        k = k_ref[0, 0, pl.ds(r0, C), :]
        v = v_ref[0, 0, pl.ds(r0, C), :]
        qb = q.astype(BF16)
        kb = k.astype(BF16)
        kk = lax.dot_general(kb, kb, dn_t, preferred_element_type=F32)
        qk_raw = lax.dot_general(qb, kb, dn_t, preferred_element_type=F32)
        for d in (0, 1):
            g_c = gcol_ref[0, 0, pl.ds(r0, C), d:d + 1]
            b_c = gcol_ref[0, 0, pl.ds(r0, C), 2 + d:3 + d]
            g_r = grow_ref[0, 0, d, pl.ds(c, 1), :]
            cum_col = jnp.sum(jnp.where(incl[d], g_r, 0.0), axis=1, keepdims=True)
            cum_row = jnp.sum(jnp.where(incl[1 - d], g_c, 0.0), axis=0, keepdims=True)
            cum_last = jnp.sum(g_r, axis=1, keepdims=True)
            decay = jnp.where(incl[d], jnp.exp(jnp.where(incl[d], cum_col - cum_row, 0.0)), 0.0)
            a_mat = jnp.where(strict[d], b_c * kk * decay, 0.0)
            t_inv = _unit_lower_inverse(a_mat)
            e_g = jnp.exp(cum_col)
            rhs = jnp.concatenate([v * b_c, k * (b_c * e_g)], axis=1)
            sol = jnp.dot(t_inv.astype(BF16), rhs.astype(BF16), preferred_element_type=F32)
            u_scr[d, pl.ds(r0, C), :] = sol[:, :GDN_HEAD_DIM]
            r1 = pl.multiple_of(c * 2 * C, 2 * C)
            wq_scr[d, pl.ds(r1, C), :] = sol[:, GDN_HEAD_DIM:].astype(BF16)
            wq_scr[d, pl.ds(r1 + C, C), :] = (q * e_g).astype(BF16)
            kd_scr[d, pl.ds(r0, C), :] = (k * jnp.exp(cum_last - cum_col)).astype(BF16)
            qk_scr[d, pl.ds(r0, C), :] = (qk_raw * decay).astype(BF16)
            cd_scr[d, pl.ds(c, 1), :] = jnp.broadcast_to(jnp.exp(cum_last), (1, GDN_HEAD_DIM))

    def pre_iter(it, carry):
        for s in range(chunks_per_iter):
            precompute(it * chunks_per_iter + s)
        return carry

    lax.fori_loop(0, n_chunks // chunks_per_iter, pre_iter, 0)

    o_ref[...] = jnp.zeros_like(o_ref)

    def step(t, states):
        c_bwd = jnp.where(t < n_ctx_chunks, n_ctx_chunks - 1 - t, n_chunks - 1 + n_ctx_chunks - t)
        new_states = []
        for d, c in ((0, t), (1, c_bwd)):
            s_mat = states[d]
            r0 = pl.multiple_of(c * C, C)
            r1 = pl.multiple_of(c * 2 * C, 2 * C)
            ws_qs = jnp.dot(wq_scr[d, pl.ds(r1, 2 * C), :], s_mat.astype(BF16), preferred_element_type=F32)
            v_new = u_scr[d, pl.ds(r0, C), :] - ws_qs[:C]
            v_new_b = v_new.astype(BF16)
            o_c = ws_qs[C:] + jnp.dot(qk_scr[d, pl.ds(r0, C), :], v_new_b, preferred_element_type=F32)
            o_ref[0, 0, pl.ds(r0, C), :] += o_c
            kd = kd_scr[d, pl.ds(r0, C), :]
            s_mat = s_mat * cd_scr[d, pl.ds(c, 1), :] + lax.dot_general(
                kd, v_new_b, (((0,), (0,)), ((), ())), preferred_element_type=F32)
            new_states.append(s_mat)
        return tuple(new_states)

    s0 = jnp.zeros((GDN_HEAD_DIM, GDN_HEAD_DIM), F32)
    lax.fori_loop(0, n_chunks, step, (s0, s0))


def gated_delta_bidirectional(q, k, v, gcol, grow, *, n_ctx_chunks):
    B, H, Lt, dk = q.shape
    n_chunks = Lt // CHUNK
    chunks_per_iter = 2
    assert n_chunks % chunks_per_iter == 0
    cd_rows = -(-n_chunks // 8) * 8
    seq_spec = pl.BlockSpec((1, 1, Lt, dk), lambda b, h: (b, h, 0, 0))
    scratch = [pltpu.VMEM((2, Lt, dk), F32),
               pltpu.VMEM((2, 2 * Lt, dk), BF16),
               pltpu.VMEM((2, Lt, dk), BF16),
               pltpu.VMEM((2, Lt, CHUNK), BF16),
               pltpu.VMEM((2, cd_rows, dk), F32)]
    est = (2 * 4 * Lt * dk * 4
           + 2 * Lt * V7X_LANES * 4
           + 2 * Lt * dk * 4 + 2 * 2 * Lt * dk * 2 + 2 * Lt * dk * 2 + 2 * Lt * V7X_LANES * 2)
    body = functools.partial(_gdn_body, n_chunks=n_chunks, n_ctx_chunks=n_ctx_chunks,
                             chunks_per_iter=chunks_per_iter)
    return pl.pallas_call(
        body,
        grid=(B, H),
        in_specs=[seq_spec, seq_spec, seq_spec,
                  pl.BlockSpec((1, 1, Lt, 4), lambda b, h: (b, h, 0, 0)),
                  pl.BlockSpec((1, 1, 2, n_chunks, CHUNK), lambda b, h: (b, h, 0, 0, 0))],
        out_specs=seq_spec,
        out_shape=jax.ShapeDtypeStruct((B, H, Lt, dk), F32),
        scratch_shapes=scratch,
        compiler_params=pltpu.CompilerParams(dimension_semantics=("parallel", "parallel"),
                                             vmem_limit_bytes=_vmem_limit(est // 2 + est // 4)),
        name="gated_delta_scan",
    )(q, k, v, gcol, grow)


def _attn_body(q_ref, k_ref, v_ref, o_ref, *, n_ctx, first_tile):
    G, tq, hd = q_ref.shape[2:]
    tile = pl.program_id(2) + first_tile
    q = q_ref[0, 0].reshape(G * tq, hd)
    s = lax.dot_general(q, k_ref[0, 0], (((1,), (1,)), ((), ())), preferred_element_type=F32)
    col = lax.broadcasted_iota(jnp.int32, s.shape, 1)
    allowed = jnp.logical_or(tile * tq >= n_ctx, col < n_ctx)
    s = jnp.where(allowed, s, -1e30)
    m = jnp.max(s, axis=1, keepdims=True)
    p = jnp.exp(s - m)
    denom = jnp.sum(p, axis=1, keepdims=True)
    o = jnp.dot(p.astype(BF16), v_ref[0, 0], preferred_element_type=F32) / denom
    for g in range(G):
        o_ref[0, :, g * hd:(g + 1) * hd] = o[g * tq:(g + 1) * tq]


def attention(q, k, v, *, n_ctx, skip_ctx_queries):
    B, Hkv, G, Lt, hd = q.shape
    tq = ATT_Q_TILE
    assert Lt % tq == 0 and n_ctx % tq == 0
    first_tile = n_ctx // tq if skip_ctx_queries else 0
    n_tiles = Lt // tq - first_tile
    est = 2 * (G * tq * hd * 2 + 2 * Lt * hd * 2 + tq * G * hd * 4) + 3 * G * tq * Lt * 4
    body = functools.partial(_attn_body, n_ctx=n_ctx, first_tile=first_tile)
    return pl.pallas_call(
        body,
        grid=(B, Hkv, n_tiles),
        in_specs=[pl.BlockSpec((1, 1, G, tq, hd), lambda b, h, i: (b, h, 0, i + first_tile, 0)),
                  pl.BlockSpec((1, 1, Lt, hd), lambda b, h, i: (b, h, 0, 0)),
                  pl.BlockSpec((1, 1, Lt, hd), lambda b, h, i: (b, h, 0, 0))],
        out_specs=pl.BlockSpec((1, tq, G * hd), lambda b, h, i: (b, i, h)),
        out_shape=jax.ShapeDtypeStruct((B, n_tiles * tq, Hkv * G * hd), F32),
        compiler_params=pltpu.CompilerParams(dimension_semantics=("parallel", "parallel", "parallel"),
                                             vmem_limit_bytes=_vmem_limit(est // 2)),
        name="gqa_attention",
    )(q, k, v)


def _expert_body(blk_e_ref, n_used_ref, x_ref, wg_ref, wu_ref, wd_ref, o_ref):
    del blk_e_ref

    @pl.when(pl.program_id(0) < n_used_ref[0])
    def _():
        x = x_ref[...]
        a = jnp.dot(x, wg_ref[0].astype(BF16), preferred_element_type=F32)
        b = jnp.dot(x, wu_ref[0].astype(BF16), preferred_element_type=F32)
        h = (a * jax.nn.sigmoid(a)) * b
        o_ref[...] = jnp.dot(h.astype(BF16), wd_ref[0].astype(BF16), preferred_element_type=F32)


def expert_mlp(xs, blk_e, n_used, w_gate, w_up, w_down):
    P, D = xs.shape
    E, _, F = w_gate.shape
    tm = EXPERT_ROWS
    n_blocks = P // tm

    def row_map(i, blk_e_ref, n_used_ref):
        return (jnp.minimum(i, n_used_ref[0] - 1), 0)

    def w_map(i, blk_e_ref, n_used_ref):
        return (blk_e_ref[i], 0, 0)

    est = 2 * (tm * D * 2 + 3 * D * F * 4 + tm * D * 4) + 3 * D * F * 2 + 3 * tm * F * 4
    grid_spec = pltpu.PrefetchScalarGridSpec(
        num_scalar_prefetch=2,
        grid=(n_blocks,),
        in_specs=[pl.BlockSpec((tm, D), row_map),
                  pl.BlockSpec((1, D, F), w_map),
                  pl.BlockSpec((1, D, F), w_map),
                  pl.BlockSpec((1, F, D), w_map)],
        out_specs=pl.BlockSpec((tm, D), row_map),
    )
    return pl.pallas_call(
        _expert_body,
        grid_spec=grid_spec,
        out_shape=jax.ShapeDtypeStruct((P, D), F32),
        compiler_params=pltpu.CompilerParams(dimension_semantics=("arbitrary",),
                                             vmem_limit_bytes=_vmem_limit(est)),
        name="expert_mlp",
    )(blk_e, n_used, xs, w_gate, w_up, w_down)


def _standardize(x, eps):
    mu = jnp.mean(x, -1, keepdims=True)
    var = jnp.mean(jnp.square(x - mu), -1, keepdims=True)
    return (x - mu) * lax.rsqrt(var + eps)


def _layer_norm(x, w, b):
    return _standardize(x, LN_EPS) * w + b


def _rms_norm(x, w):
    return x * lax.rsqrt(jnp.mean(x * x, -1, keepdims=True) + RMS_EPS) * w


def _l2_normalize(x):
    return x * lax.rsqrt(jnp.sum(x * x, -1, keepdims=True) + RMS_EPS)


def _rope_tables(n_ctx, n_lat):
    rows = n_lat // GRID_W
    row = jnp.repeat(jnp.arange(rows, dtype=F32), GRID_W)
    col = jnp.tile(jnp.arange(GRID_W, dtype=F32), rows)
    inv_freq = ROPE_THETA ** (-jnp.arange(0, ROPE_AXIS_DIM, 2, dtype=F32) / ROPE_AXIS_DIM)
    ang = jnp.stack([row[:, None] * inv_freq, col[:, None] * inv_freq], axis=1)
    cos = jnp.concatenate([jnp.ones((n_ctx,) + ang.shape[1:], F32), jnp.cos(ang)], axis=0)
    sin = jnp.concatenate([jnp.zeros((n_ctx,) + ang.shape[1:], F32), jnp.sin(ang)], axis=0)
    return cos, sin


def _apply_rope(x, cos, sin):
    B, Lt, H, hd = x.shape
    F = ROPE_AXIS_DIM // 2
    xf = x.reshape(B, Lt, H, 2, 2, F)
    x1, x2 = xf[..., 0, :], xf[..., 1, :]
    c, s = cos[None, :, None], sin[None, :, None]
    return jnp.stack([x1 * c - x2 * s, x2 * c + x1 * s], axis=-2).reshape(B, Lt, H, hd)


def _segment_conv(x, w, n_ctx):
    def conv(seg):
        n = seg.shape[1]
        pad = jnp.pad(seg, ((0, 0), (CONV_W // 2, CONV_W // 2), (0, 0)))
        return sum(pad[:, i:i + n] * w[i] for i in range(CONV_W))
    return jnp.concatenate([conv(x[:, :n_ctx]), conv(x[:, n_ctx:])], axis=1)


def _mixer(h_bf, w_main, w_ba, conv_w, a_log, dt_bias, gdn_norm_w, q_norm_w, k_norm_w, w_out_bf, cos, sin,
           *, B, Lt, n_ctx, skip_ctx_queries):
    T = B * Lt
    p = matmul(h_bf, w_main, tm=MM_ROWS, tn=1024)
    p_ba = matmul(h_bf, w_ba, tm=MM_ROWS, tn=V7X_LANES)
    p = p.reshape(B, Lt, -1)
    qkv = jax.nn.silu(_segment_conv(p[..., :OFF_Z], conv_w, n_ctx))
    heads = lambda t: t.reshape(B, Lt, GDN_HEADS, GDN_HEAD_DIM).transpose(0, 2, 1, 3)
    q, k, v = (heads(qkv[..., i * GDN_WIDTH:(i + 1) * GDN_WIDTH]) for i in range(3))
    q = _l2_normalize(q) * GDN_HEAD_DIM ** -0.5
    k = _l2_normalize(k)
    ba = p_ba[:, :4 * GDN_HEADS].reshape(B, Lt, 2, 2, GDN_HEADS)
    beta = jax.nn.sigmoid(ba[:, :, 0])
    g = -jnp.exp(a_log) * jax.nn.softplus(ba[:, :, 1] + dt_bias)
    gcol = jnp.concatenate([g, beta], axis=2).transpose(0, 3, 1, 2)
    grow = g.transpose(0, 3, 2, 1).reshape(B, GDN_HEADS, 2, Lt // CHUNK, CHUNK)
    o = gated_delta_bidirectional(q, k, v, gcol, grow, n_ctx_chunks=n_ctx // CHUNK)
    z = p[..., OFF_Z:OFF_BA].reshape(B, Lt, GDN_HEADS, GDN_HEAD_DIM)
    gdn = (_rms_norm(o.transpose(0, 2, 1, 3), gdn_norm_w) * jax.nn.silu(z)).reshape(B, Lt, GDN_WIDTH)
    a = p[..., OFF_BA:]
    qa = _rms_norm(a[..., :ATT_WIDTH].reshape(B, Lt, ATT_Q_HEADS, ATT_HEAD_DIM), q_norm_w)
    ka = _rms_norm(a[..., ATT_WIDTH:ATT_WIDTH + ATT_KV_WIDTH].reshape(B, Lt, ATT_KV_HEADS, ATT_HEAD_DIM), k_norm_w)
    va = a[..., ATT_WIDTH + ATT_KV_WIDTH:].reshape(B, Lt, ATT_KV_HEADS, ATT_HEAD_DIM)
    qa = _apply_rope(qa, cos, sin) * ATT_HEAD_DIM ** -0.5
    ka = _apply_rope(ka, cos, sin)
    qa = qa.astype(BF16).reshape(B, Lt, ATT_KV_HEADS, ATT_GROUP, ATT_HEAD_DIM).transpose(0, 2, 3, 1, 4)
    ka = ka.astype(BF16).transpose(0, 2, 1, 3)
    va = va.astype(BF16).transpose(0, 2, 1, 3)
    att = attention(qa, ka, va, n_ctx=n_ctx, skip_ctx_queries=skip_ctx_queries)
    if skip_ctx_queries:
        gdn = gdn[:, n_ctx:]
    mix = jnp.concatenate([gdn, att], axis=-1).astype(BF16)
    return matmul(mix.reshape(-1, mix.shape[-1]), w_out_bf, tm=MM_ROWS, tn=1024)


def _route(logits, router_bias):
    T = logits.shape[0]
    s = jax.nn.sigmoid(logits)
    sel = s + router_bias
    per_group = N_EXPERTS // N_GROUPS
    grp_score = jnp.sum(lax.top_k(sel.reshape(T, N_GROUPS, per_group), 2)[0], axis=-1)
    _, top_groups = lax.top_k(grp_score, TOPK_GROUPS)
    group_mask = jnp.any(top_groups[:, :, None] == jnp.arange(N_GROUPS)[None, None, :], axis=1)
    expert_mask = jnp.repeat(group_mask, per_group, axis=1)
    _, eidx = lax.top_k(jnp.where(expert_mask, sel, -jnp.inf), TOP_K)
    wts = jnp.take_along_axis(s, eidx, axis=1)
    wts = wts / jnp.sum(wts, -1, keepdims=True) * ROUTED_SCALE
    return eidx, wts


def _moe(h_bf, router_w_bf, router_bias, w_gate, w_up, w_down, sh_gate, sh_up, sh_down):
    T, D = h_bf.shape
    E = N_EXPERTS
    tm = EXPERT_ROWS
    logits = matmul(h_bf, router_w_bf, tm=MM_ROWS, tn=E)
    eidx, wts = _route(logits, router_bias)
    onehot = jnp.any(eidx[:, :, None] == jnp.arange(E, dtype=eidx.dtype)[None, None, :], axis=1).astype(jnp.int32)
    cum = jnp.cumsum(onehot, axis=0)
    counts = cum[-1]
    rank = jnp.take_along_axis(cum - onehot, eidx, axis=1)
    padded = (counts + tm - 1) // tm * tm
    pad_ends = jnp.cumsum(padded)
    pad_starts = pad_ends - padded
    dest = pad_starts[eidx] + rank
    n_blocks = -(-T * TOP_K // tm) + E
    P = n_blocks * tm
    n_used = (pad_ends[-1] // tm).astype(jnp.int32)
    blk = jnp.arange(n_blocks, dtype=jnp.int32)
    blk_e = jnp.searchsorted(pad_ends, jnp.minimum(blk, n_used - 1) * tm, side='right').astype(jnp.int32)
    blk_e = jnp.minimum(blk_e, E - 1)
    flat_tok = jnp.repeat(jnp.arange(T, dtype=jnp.int32), TOP_K)
    buf_tok = jnp.zeros((P,), jnp.int32).at[dest.reshape(-1)].set(flat_tok, unique_indices=True)
    xs = jnp.take(h_bf, buf_tok, axis=0)
    out_sorted = expert_mlp(xs, blk_e, n_used.reshape(1), w_gate, w_up, w_down)
    rows = jnp.take(out_sorted, dest.reshape(-1), axis=0).reshape(T, TOP_K, D)
    routed = jnp.sum(rows * wts[:, :, None], axis=1)
    n_sh = T // tm
    shared = expert_mlp(h_bf, jnp.zeros((n_sh,), jnp.int32), jnp.full((1,), n_sh, jnp.int32),
                        sh_gate[None], sh_up[None], sh_down[None])
    return routed + shared


def kernel(x, c, ctx, c_ctx, ada_w, ada_b, w_in, conv_w, gdn_a_log, gdn_dt_bias, gdn_norm_w, q_norm_w, k_norm_w,
           w_out, ln1_w, ln1_b, router_w, router_bias, exp_w_gate, exp_w_up, exp_w_down, sh_w_gate, sh_w_up,
           sh_w_down, ln2_w, ln2_b):
    B, L, D = x.shape
    Lc = ctx.shape[1]
    Lt = Lc + L
    depth = ada_w.shape[0]
    alpha = (2.0 * depth) ** 0.25
    cos, sin = _rope_tables(Lc, L)
    xa = _standardize(jnp.concatenate([ctx, x], axis=1), LN_EPS)
    is_ctx = (jnp.arange(Lt) < Lc)[None, :, None]
    cond = jnp.concatenate([c, c_ctx[None], jnp.zeros((16 - B - 1, D), F32)], axis=0)
    cond = jax.nn.silu(cond)
    for l in range(depth):
        last = l == depth - 1
        mod_all = matmul(cond, ada_w[l], tm=16, tn=1024) + ada_b[l]
        mod, mod_c = mod_all[:B], mod_all[B]
        pick = lambda i: jnp.where(is_ctx, mod_c[None, None, i * D:(i + 1) * D], mod[:, None, i * D:(i + 1) * D])
        sh1, sc1, g1, sh2, sc2, g2 = (pick(i) for i in range(6))
        w_l = w_in[l]
        w_main = jnp.concatenate([w_l[:, :OFF_BA], w_l[:, OFF_ATT:]], axis=1).astype(BF16)
        w_ba = jnp.pad(w_l[:, OFF_BA:OFF_ATT], ((0, 0), (0, V7X_LANES - 4 * GDN_HEADS))).astype(BF16)
        h = (xa * (1.0 + sc1) + sh1).astype(BF16).reshape(B * Lt, D)
        y = _mixer(h, w_main, w_ba, conv_w[l], gdn_a_log[l], gdn_dt_bias[l], gdn_norm_w[l], q_norm_w[l],
                   k_norm_w[l], w_out[l].astype(BF16), cos, sin, B=B, Lt=Lt, n_ctx=Lc, skip_ctx_queries=last)
        if last:
            xa, g1, sc2, sh2, g2 = (t[:, Lc:] for t in (xa, g1, sc2, sh2, g2))
        rows = xa.shape[1]
        xa = _layer_norm(alpha * xa + g1 * y.reshape(B, rows, D), ln1_w[l], ln1_b[l])
        h2 = (xa * (1.0 + sc2) + sh2).astype(BF16).reshape(B * rows, D)
        ff = _moe(h2, router_w[l].astype(BF16), router_bias[l], exp_w_gate[l], exp_w_up[l], exp_w_down[l],
                  sh_w_gate[l], sh_w_up[l], sh_w_down[l])
        xa = _layer_norm(alpha * xa + g2 * ff.reshape(B, rows, D), ln2_w[l], ln2_b[l])
    return xa
```

```python
import functools

import jax
import jax.numpy as jnp
from jax import lax
from jax.experimental import pallas as pl
from jax.experimental.pallas import tpu as pltpu

F32 = jnp.float32
BF16 = jnp.bfloat16

GRID_W = 64
GDN_HEAD_DIM = 128
GDN_HEADS = 4
GDN_WIDTH = GDN_HEADS * GDN_HEAD_DIM
CONV_W = 5
CHUNK = 64
ATT_HEAD_DIM = 128
ATT_Q_HEADS = 4
ATT_KV_HEADS = 2
ATT_GROUP = ATT_Q_HEADS // ATT_KV_HEADS
ATT_WIDTH = ATT_Q_HEADS * ATT_HEAD_DIM
ATT_KV_WIDTH = ATT_KV_HEADS * ATT_HEAD_DIM
ROPE_AXIS_DIM = ATT_HEAD_DIM // 2
ROPE_THETA = 10000.0
OFF_Z = 3 * GDN_WIDTH
OFF_BA = 4 * GDN_WIDTH
OFF_ATT = OFF_BA + 4 * GDN_HEADS
N_EXPERTS = 256
TOP_K = 8
N_GROUPS = 8
TOPK_GROUPS = 4
EXPERT_DIM = 256
ROUTED_SCALE = 2.5
LN_EPS = 1e-5
RMS_EPS = 1e-6

V7X_LANES = 128
V7X_SUBLANES = 8
V7X_VMEM_BYTES = 64 * 1024 * 1024
VMEM_CAP = V7X_VMEM_BYTES - 8 * 1024 * 1024

EXPERT_ROWS = 256
GDN_CHUNKS_PER_ITER = 4
ROUTER_ROWS = 512
DISPATCH_TOKENS = 256
COMBINE_TOKENS = 128
DMA_TOKENS_PER_ITER = 2
ATT_Q_TILE = 256
MM_ROWS = 1024


def _vmem_limit(estimate_bytes):
    return int(min(VMEM_CAP, max(16 * 1024 * 1024, 2 * estimate_bytes)))


def _mm_body(a_ref, w_ref, o_ref):
    o_ref[...] = jnp.dot(a_ref[...].astype(BF16), w_ref[...].astype(BF16),
                         preferred_element_type=F32).astype(o_ref.dtype)


def matmul(a, w, *, tm, tn, out_dtype=F32):
    M, K = a.shape
    N = w.shape[1]
    assert M % tm == 0 and N % tn == 0, (M, N, tm, tn)
    est = 2 * (tm * K * a.dtype.itemsize + K * tn * w.dtype.itemsize + tm * tn * 4) + tm * K * 2 + K * tn * 2
    return pl.pallas_call(
        _mm_body,
        grid=(M // tm, N // tn),
        in_specs=[pl.BlockSpec((tm, K), lambda i, j: (i, 0)),
                  pl.BlockSpec((K, tn), lambda i, j: (0, j))],
        out_specs=pl.BlockSpec((tm, tn), lambda i, j: (i, j)),
        out_shape=jax.ShapeDtypeStruct((M, N), out_dtype),
        compiler_params=pltpu.CompilerParams(dimension_semantics=("parallel", "parallel"),
                                             vmem_limit_bytes=_vmem_limit(est)),
        name="proj_matmul",
    )(a, w)


def _unit_lower_inverses(a_list):
    n = a_list[0].shape[0]
    ii = lax.broadcasted_iota(jnp.int32, (n, n), 0)
    jj = lax.broadcasted_iota(jnp.int32, (n, n), 1)
    eye = jnp.where(ii == jj, 1.0, 0.0)
    mm = lambda a, b: jnp.dot(a.astype(BF16), b.astype(BF16), preferred_element_type=F32)
    pows = [-a for a in a_list]
    prods = [eye + p for p in pows]
    for _ in range(n.bit_length() - 2):
        pows = [mm(p, p) for p in pows]
        prods = [pr + mm(pr, p) for pr, p in zip(prods, pows)]
    return prods


def _gdn_body(q_ref, k_ref, v_ref, gcol_ref, grow_ref, o_ref,
              u_scr, wq_scr, kd_scr, qk_scr, cd_scr, *, n_chunks, n_ctx_chunks, chunks_per_iter):
    C = CHUNK
    dn_t = (((1,), (1,)), ((), ()))
    ii = lax.broadcasted_iota(jnp.int32, (C, C), 0)
    jj = lax.broadcasted_iota(jnp.int32, (C, C), 1)
    incl = (ii >= jj, ii <= jj)
    strict = (ii > jj, ii < jj)

    def pre_iter(it, carry):
        chunks = [it * chunks_per_iter + s for s in range(chunks_per_iter)]
        rows = [pl.multiple_of(c * C, C) for c in chunks]
        qs = [q_ref[0, 0, pl.ds(r0, C), :] for r0 in rows]
        ks = [k_ref[0, 0, pl.ds(r0, C), :] for r0 in rows]
        vs = [v_ref[0, 0, pl.ds(r0, C), :] for r0 in rows]
        kbs = [k.astype(BF16) for k in ks]
        kks = [lax.dot_general(kb, kb, dn_t, preferred_element_type=F32) for kb in kbs]
        qk_raws = [lax.dot_general(q.astype(BF16), kb, dn_t, preferred_element_type=F32) for q, kb in zip(qs, kbs)]
        combos = [(s, d) for s in range(chunks_per_iter) for d in (0, 1)]
        terms = []
        for s, d in combos:
            r0, c = rows[s], chunks[s]
            g_c = gcol_ref[0, 0, pl.ds(r0, C), d:d + 1]
            b_c = gcol_ref[0, 0, pl.ds(r0, C), 2 + d:3 + d]
            g_r = grow_ref[0, 0, d, pl.ds(c, 1), :]
            cum_col = jnp.sum(jnp.where(incl[d], g_r, 0.0), axis=1, keepdims=True)
            cum_row = jnp.sum(jnp.where(incl[1 - d], g_c, 0.0), axis=0, keepdims=True)
            cum_last = jnp.sum(g_r, axis=1, keepdims=True)
            decay = jnp.where(incl[d], jnp.exp(jnp.where(incl[d], cum_col - cum_row, 0.0)), 0.0)
            a_mat = jnp.where(strict[d], b_c * kks[s] * decay, 0.0)
            terms.append((b_c, cum_col, cum_last, decay, a_mat))
        t_invs = _unit_lower_inverses([t[4] for t in terms])
        sols = []
        for (s, d), (b_c, cum_col, cum_last, decay, _), t_inv in zip(combos, terms, t_invs):
            e_g = jnp.exp(cum_col)
            rhs = jnp.concatenate([vs[s] * b_c, ks[s] * (b_c * e_g)], axis=1)
            sols.append((e_g, jnp.dot(t_inv.astype(BF16), rhs.astype(BF16), preferred_element_type=F32)))
        for (s, d), (b_c, cum_col, cum_last, decay, _), (e_g, sol) in zip(combos, terms, sols):
            r0, c = rows[s], chunks[s]
            r1 = pl.multiple_of(c * 2 * C, 2 * C)
            u_scr[d, pl.ds(r0, C), :] = sol[:, :GDN_HEAD_DIM]
            wq_scr[d, pl.ds(r1, C), :] = sol[:, GDN_HEAD_DIM:].astype(BF16)
            wq_scr[d, pl.ds(r1 + C, C), :] = (qs[s] * e_g).astype(BF16)
            kd_scr[d, pl.ds(r0, C), :] = (ks[s] * jnp.exp(cum_last - cum_col)).astype(BF16)
            qk_scr[d, pl.ds(r0, C), :] = (qk_raws[s] * decay).astype(BF16)
            cd_scr[d, pl.ds(c, 1), :] = jnp.broadcast_to(jnp.exp(cum_last), (1, GDN_HEAD_DIM))
        return carry

    lax.fori_loop(0, n_chunks // chunks_per_iter, pre_iter, 0)

    o_ref[...] = jnp.zeros_like(o_ref)

    def step(t, states):
        c_bwd = jnp.where(t < n_ctx_chunks, n_ctx_chunks - 1 - t, n_chunks - 1 + n_ctx_chunks - t)
        chunks = (t, c_bwd)
        rows = [pl.multiple_of(c * C, C) for c in chunks]
        ws_qs = [jnp.dot(wq_scr[d, pl.ds(pl.multiple_of(chunks[d] * 2 * C, 2 * C), 2 * C), :],
                         states[d].astype(BF16), preferred_element_type=F32) for d in (0, 1)]
        v_new_b = [(u_scr[d, pl.ds(rows[d], C), :] - ws_qs[d][:C]).astype(BF16) for d in (0, 1)]
        intra = [jnp.dot(qk_scr[d, pl.ds(rows[d], C), :], v_new_b[d], preferred_element_type=F32) for d in (0, 1)]
        upd = [lax.dot_general(kd_scr[d, pl.ds(rows[d], C), :], v_new_b[d], (((0,), (0,)), ((), ())),
                               preferred_element_type=F32) for d in (0, 1)]
        for d in (0, 1):
            o_ref[0, 0, pl.ds(rows[d], C), :] += ws_qs[d][C:] + intra[d]
        return tuple(states[d] * cd_scr[d, pl.ds(chunks[d], 1), :] + upd[d] for d in (0, 1))

    s0 = jnp.zeros((GDN_HEAD_DIM, GDN_HEAD_DIM), F32)
    lax.fori_loop(0, n_chunks, step, (s0, s0))


def gated_delta_bidirectional(q, k, v, gcol, grow, *, n_ctx_chunks):
    B, H, Lt, dk = q.shape
    n_chunks = Lt // CHUNK
    chunks_per_iter = GDN_CHUNKS_PER_ITER
    assert n_chunks % chunks_per_iter == 0
    cd_rows = -(-n_chunks // 8) * 8
    seq_spec = pl.BlockSpec((1, 1, Lt, dk), lambda b, h: (b, h, 0, 0))
    scratch = [pltpu.VMEM((2, Lt, dk), F32),
               pltpu.VMEM((2, 2 * Lt, dk), BF16),
               pltpu.VMEM((2, Lt, dk), BF16),
               pltpu.VMEM((2, Lt, CHUNK), BF16),
               pltpu.VMEM((2, cd_rows, dk), F32)]
    est = (2 * 4 * Lt * dk * 4
           + 2 * Lt * V7X_LANES * 4
           + 2 * Lt * dk * 4 + 2 * 2 * Lt * dk * 2 + 2 * Lt * dk * 2 + 2 * Lt * V7X_LANES * 2)
    body = functools.partial(_gdn_body, n_chunks=n_chunks, n_ctx_chunks=n_ctx_chunks,
                             chunks_per_iter=chunks_per_iter)
    return pl.pallas_call(
        body,
        grid=(B, H),
        in_specs=[seq_spec, seq_spec, seq_spec,
                  pl.BlockSpec((1, 1, Lt, 4), lambda b, h: (b, h, 0, 0)),
                  pl.BlockSpec((1, 1, 2, n_chunks, CHUNK), lambda b, h: (b, h, 0, 0, 0))],
        out_specs=seq_spec,
        out_shape=jax.ShapeDtypeStruct((B, H, Lt, dk), F32),
        scratch_shapes=scratch,
        compiler_params=pltpu.CompilerParams(dimension_semantics=("parallel", "parallel"),
                                             vmem_limit_bytes=_vmem_limit(est // 2 + est // 4)),
        name="gated_delta_scan",
    )(q, k, v, gcol, grow)


def _attn_body(q_ref, k_ref, v_ref, o_ref, *, n_ctx, first_tile):
    G, tq, hd = q_ref.shape[2:]
    tile = pl.program_id(2) + first_tile
    q = q_ref[0, 0].reshape(G * tq, hd)
    s = lax.dot_general(q, k_ref[0, 0], (((1,), (1,)), ((), ())), preferred_element_type=F32)
    col = lax.broadcasted_iota(jnp.int32, s.shape, 1)
    allowed = jnp.logical_or(tile * tq >= n_ctx, col < n_ctx)
    s = jnp.where(allowed, s, -1e30)
    m = jnp.max(s, axis=1, keepdims=True)
    p = jnp.exp(s - m)
    denom = jnp.sum(p, axis=1, keepdims=True)
    o = jnp.dot(p.astype(BF16), v_ref[0, 0], preferred_element_type=F32) / denom
    for g in range(G):
        o_ref[0, :, g * hd:(g + 1) * hd] = o[g * tq:(g + 1) * tq]


def attention(q, k, v, *, n_ctx, skip_ctx_queries):
    B, Hkv, G, Lt, hd = q.shape
    tq = ATT_Q_TILE
    assert Lt % tq == 0 and n_ctx % tq == 0
    first_tile = n_ctx // tq if skip_ctx_queries else 0
    n_tiles = Lt // tq - first_tile
    est = 2 * (G * tq * hd * 2 + 2 * Lt * hd * 2 + tq * G * hd * 4) + 3 * G * tq * Lt * 4
    body = functools.partial(_attn_body, n_ctx=n_ctx, first_tile=first_tile)
    return pl.pallas_call(
        body,
        grid=(B, Hkv, n_tiles),
        in_specs=[pl.BlockSpec((1, 1, G, tq, hd), lambda b, h, i: (b, h, 0, i + first_tile, 0)),
                  pl.BlockSpec((1, 1, Lt, hd), lambda b, h, i: (b, h, 0, 0)),
                  pl.BlockSpec((1, 1, Lt, hd), lambda b, h, i: (b, h, 0, 0))],
        out_specs=pl.BlockSpec((1, tq, G * hd), lambda b, h, i: (b, i, h)),
        out_shape=jax.ShapeDtypeStruct((B, n_tiles * tq, Hkv * G * hd), F32),
        compiler_params=pltpu.CompilerParams(dimension_semantics=("parallel", "parallel", "parallel"),
                                             vmem_limit_bytes=_vmem_limit(est // 2)),
        name="gqa_attention",
    )(q, k, v)


def _rows_2d(x_ref):
    tm = x_ref.shape[0] // V7X_SUBLANES
    return jnp.concatenate([x_ref[pl.ds(c, tm, stride=V7X_SUBLANES), :] for c in range(V7X_SUBLANES)], axis=1)


def _router_body(h_ref, w_ref, bias_ref, eidx_ref, wts_ref, rank_ref, counts_ref, carry_scr):
    tm = h_ref.shape[0] // V7X_SUBLANES
    E = N_EXPERTS
    per_group = E // N_GROUPS
    neg = -jnp.inf

    @pl.when(pl.program_id(0) == 0)
    def _():
        carry_scr[...] = jnp.zeros_like(carry_scr)

    logits = jnp.dot(_rows_2d(h_ref).astype(BF16), w_ref[...], preferred_element_type=F32)
    s = jax.nn.sigmoid(logits)
    sel = s + bias_ref[...]
    lane = lax.broadcasted_iota(jnp.int32, (tm, E), 1)
    lane_f = lane.astype(F32)
    grp = lane // per_group
    scores = []
    for g in range(N_GROUPS):
        v = jnp.where(grp == g, sel, neg)
        m1 = jnp.max(v, axis=1, keepdims=True)
        i1 = jnp.min(jnp.where(v == m1, lane_f, float(E)), axis=1, keepdims=True)
        m2 = jnp.max(jnp.where(lane_f == i1, neg, v), axis=1, keepdims=True)
        scores.append(m1 + m2)
    allowed = jnp.zeros((tm, E), jnp.bool_)
    for g in range(N_GROUPS):
        beaten = jnp.zeros((tm, 1), F32)
        for o in range(N_GROUPS):
            if o != g:
                wins = (scores[o] > scores[g]) if o > g else (scores[o] >= scores[g])
                beaten = beaten + wins.astype(F32)
        allowed = jnp.logical_or(allowed, jnp.logical_and(grp == g, beaten < TOPK_GROUPS))
    masked = jnp.where(allowed, sel, neg)
    idx_cols, s_cols = [], []
    chosen = jnp.zeros((tm, E), jnp.bool_)
    for _ in range(TOP_K):
        m = jnp.max(masked, axis=1, keepdims=True)
        idx = jnp.min(jnp.where(masked == m, lane_f, float(E)), axis=1, keepdims=True)
        hit = lane_f == idx
        s_cols.append(jnp.sum(jnp.where(hit, s, 0.0), axis=1, keepdims=True))
        idx_cols.append(idx)
        chosen = jnp.logical_or(chosen, hit)
        masked = jnp.where(hit, neg, masked)
    ri = lax.broadcasted_iota(jnp.int32, (tm, tm), 0)
    ci = lax.broadcasted_iota(jnp.int32, (tm, tm), 1)
    running = jnp.dot((ci < ri).astype(BF16), chosen.astype(BF16), preferred_element_type=F32) + carry_scr[...]
    total = s_cols[0]
    for col in s_cols[1:]:
        total = total + col
    k_lane = lax.broadcasted_iota(jnp.int32, (tm, TOP_K), 1)
    eidx = jnp.zeros((tm, TOP_K), F32)
    wts = jnp.zeros((tm, TOP_K), F32)
    rank = jnp.zeros((tm, TOP_K), F32)
    for j in range(TOP_K):
        r_j = jnp.sum(jnp.where(lane_f == idx_cols[j], running, 0.0), axis=1, keepdims=True)
        eidx = jnp.where(k_lane == j, idx_cols[j], eidx)
        wts = jnp.where(k_lane == j, s_cols[j] / total * ROUTED_SCALE, wts)
        rank = jnp.where(k_lane == j, r_j, rank)
    eidx_ref[...] = eidx.astype(jnp.int32)
    wts_ref[...] = wts
    rank_ref[...] = rank.astype(jnp.int32)
    carry_scr[...] = carry_scr[...] + jnp.sum(chosen.astype(F32), axis=0, keepdims=True)
    counts_ref[...] = carry_scr[...].astype(jnp.int32)


def moe_router(h, router_w_bf, router_bias):
    T = h.shape[0]
    D, E = router_w_bf.shape
    tm = ROUTER_ROWS
    assert T % tm == 0
    tok = pl.BlockSpec((tm, TOP_K), lambda i: (i, 0))
    est = 2 * (tm * D * 4 + D * E * 2) + tm * tm * 4 + 24 * tm * E * 4
    h = h.reshape(T * V7X_SUBLANES, V7X_LANES)
    return pl.pallas_call(
        _router_body,
        grid=(T // tm,),
        in_specs=[pl.BlockSpec((tm * V7X_SUBLANES, V7X_LANES), lambda i: (i, 0)),
                  pl.BlockSpec((D, E), lambda i: (0, 0)),
                  pl.BlockSpec((1, E), lambda i: (0, 0))],
        out_specs=[tok, tok, tok, pl.BlockSpec((1, E), lambda i: (0, 0))],
        out_shape=[jax.ShapeDtypeStruct((T, TOP_K), jnp.int32), jax.ShapeDtypeStruct((T, TOP_K), F32),
                   jax.ShapeDtypeStruct((T, TOP_K), jnp.int32), jax.ShapeDtypeStruct((1, E), jnp.int32)],
        scratch_shapes=[pltpu.VMEM((1, E), F32)],
        compiler_params=pltpu.CompilerParams(dimension_semantics=("arbitrary",),
                                             vmem_limit_bytes=_vmem_limit(est // 2)),
        name="moe_router",
    )(h, router_w_bf, router_bias.reshape(1, E))


def _dispatch_body(pstart_ref, pad_lo_ref, pad_hi_ref, eidx_ref, rank_ref, h_hbm, zero_hbm, xs_hbm, sems,
                   *, tt, n_tiles):
    i = pl.program_id(0)

    def row_copy(src, dst_row, sem):
        return pltpu.make_async_copy(src, xs_hbm.at[dst_row], sem)

    @pl.when(i == 0)
    def _():
        for wait in (False, True):
            def per_expert(e, carry, wait=wait):
                def per_row(r, c):
                    cp = row_copy(zero_hbm.at[0], r, sems.at[2])
                    cp.wait() if wait else cp.start()
                    return c
                return lax.fori_loop(pad_lo_ref[e], pad_hi_ref[e], per_row, carry)
            lax.fori_loop(0, N_EXPERTS, per_expert, 0)

    @pl.when(i < n_tiles)
    def _():
        base = i * tt
        sem = sems.at[i % 2]

        def issue(p, c):
            toks = [p * DMA_TOKENS_PER_ITER + u for u in range(DMA_TOKENS_PER_ITER)]
            dst = [[pstart_ref[eidx_ref[r * TOP_K + j]] + rank_ref[r * TOP_K + j] for j in range(TOP_K)]
                   for r in toks]
            for r, dst_r in zip(toks, dst):
                for d in dst_r:
                    row_copy(h_hbm.at[base + r], d, sem).start()
            return c
        lax.fori_loop(0, tt // DMA_TOKENS_PER_ITER, issue, 0)

    @pl.when(i >= 1)
    def _():
        sem = sems.at[(i + 1) % 2]

        def drain(r, c):
            for j in range(TOP_K):
                row_copy(h_hbm.at[0], 0, sem).wait()
            return c
        lax.fori_loop(0, tt, drain, 0)


def moe_dispatch(h, eidx_flat, rank_flat, pad_starts, pad_lo, pad_hi, n_rows):
    T = h.shape[0]
    tt = DISPATCH_TOKENS
    assert T % tt == 0
    n_tiles = T // tt
    idx_spec = pl.BlockSpec((tt * TOP_K,), lambda i, *_: (jnp.minimum(i, n_tiles - 1),), memory_space=pltpu.SMEM)
    grid_spec = pltpu.PrefetchScalarGridSpec(
        num_scalar_prefetch=3, grid=(n_tiles + 1,),
        in_specs=[idx_spec, idx_spec, pl.BlockSpec(memory_space=pl.ANY), pl.BlockSpec(memory_space=pl.ANY)],
        out_specs=pl.BlockSpec(memory_space=pl.ANY),
        scratch_shapes=[pltpu.SemaphoreType.DMA((3,))])
    return pl.pallas_call(
        functools.partial(_dispatch_body, tt=tt, n_tiles=n_tiles),
        grid_spec=grid_spec,
        out_shape=jax.ShapeDtypeStruct((n_rows,) + h.shape[1:], h.dtype),
        compiler_params=pltpu.CompilerParams(dimension_semantics=("arbitrary",)),
        name="moe_dispatch",
    )(pad_starts, pad_lo, pad_hi, eidx_flat, rank_flat, h, jnp.zeros((1,) + h.shape[1:], h.dtype))


def _combine_body(pstart_ref, eidx_ref, rank_ref, wts_ref, shared_ref, rows_hbm, o_ref, buf, sems, *, tt, n_tiles):
    i = pl.program_id(0)

    def row_copy(src_row, slot, j, r):
        return pltpu.make_async_copy(rows_hbm.at[src_row], buf.at[slot, j, r], sems.at[slot])

    @pl.when(i < n_tiles)
    def _():
        slot = i % 2

        def issue(p, c):
            toks = [p * DMA_TOKENS_PER_ITER + u for u in range(DMA_TOKENS_PER_ITER)]
            src = [[pstart_ref[eidx_ref[r * TOP_K + j]] + rank_ref[r * TOP_K + j] for j in range(TOP_K)]
                   for r in toks]
            for r, src_r in zip(toks, src):
                for j, s in enumerate(src_r):
                    row_copy(s, slot, j, r).start()
            return c
        lax.fori_loop(0, tt // DMA_TOKENS_PER_ITER, issue, 0)

    @pl.when(i >= 1)
    def _():
        slot = (i + 1) % 2

        def drain(r, c):
            for j in range(TOP_K):
                row_copy(0, slot, j, 0).wait()
            return c
        lax.fori_loop(0, tt, drain, 0)

        def per_token(r, c):
            acc = shared_ref[r]
            for j in range(TOP_K):
                acc = acc + buf[slot, j, r] * wts_ref[r * TOP_K + j]
            o_ref[r] = acc
            return c
        lax.fori_loop(0, tt, per_token, 0, unroll=4)


def moe_combine(rows, eidx_flat, rank_flat, pad_starts, wts_flat, shared):
    T = shared.shape[0]
    tile = shared.shape[1:]
    tt = COMBINE_TOKENS
    assert T % tt == 0
    n_tiles = T // tt
    prev = lambda i, *_: (jnp.maximum(i - 1, 0), 0, 0)
    idx_spec = pl.BlockSpec((tt * TOP_K,), lambda i, *_: (jnp.minimum(i, n_tiles - 1),), memory_space=pltpu.SMEM)
    wts_spec = pl.BlockSpec((tt * TOP_K,), lambda i, *_: (jnp.maximum(i - 1, 0),), memory_space=pltpu.SMEM)
    grid_spec = pltpu.PrefetchScalarGridSpec(
        num_scalar_prefetch=1, grid=(n_tiles + 1,),
        in_specs=[idx_spec, idx_spec, wts_spec, pl.BlockSpec((tt,) + tile, prev),
                  pl.BlockSpec(memory_space=pl.ANY)],
        out_specs=pl.BlockSpec((tt,) + tile, prev),
        scratch_shapes=[pltpu.VMEM((2, TOP_K, tt) + tile, F32), pltpu.SemaphoreType.DMA((2,))])
    row_bytes = tile[0] * tile[1] * 4
    est = 2 * TOP_K * tt * row_bytes + 4 * tt * row_bytes
    return pl.pallas_call(
        functools.partial(_combine_body, tt=tt, n_tiles=n_tiles),
        grid_spec=grid_spec,
        out_shape=jax.ShapeDtypeStruct((T,) + tile, F32),
        compiler_params=pltpu.CompilerParams(dimension_semantics=("arbitrary",),
                                             vmem_limit_bytes=_vmem_limit(est)),
        name="moe_combine",
    )(pad_starts, eidx_flat, rank_flat, wts_flat, shared, rows)


def _expert_body(blk_e_ref, n_used_ref, x_ref, wg_ref, wu_ref, wd_ref, o_ref, wg_b, wu_b, wd_b):
    i = pl.program_id(0)

    @pl.when(jnp.logical_or(i == 0, blk_e_ref[i] != blk_e_ref[jnp.maximum(i - 1, 0)]))
    def _():
        wg_b[...] = wg_ref[0, 0].astype(BF16)
        wu_b[...] = wu_ref[0, 0].astype(BF16)
        wd_b[...] = wd_ref[0, 0].astype(BF16)

    @pl.when(i < n_used_ref[0])
    def _():
        x = _rows_2d(x_ref).astype(BF16)
        a = jnp.dot(x, wg_b[...], preferred_element_type=F32)
        b = jnp.dot(x, wu_b[...], preferred_element_type=F32)
        h = (a * jax.nn.sigmoid(a)) * b
        o = jnp.dot(h.astype(BF16), wd_b[...], preferred_element_type=F32)
        tm = o.shape[0]
        for c in range(V7X_SUBLANES):
            o_ref[pl.ds(c, tm, stride=V7X_SUBLANES), :] = o[:, c * V7X_LANES:(c + 1) * V7X_LANES]


def expert_mlp(xs, blk_e, n_used, layer, w_gate, w_up, w_down):
    P = xs.shape[0]
    tile = xs.shape[1:]
    D = tile[0] * tile[1]
    F = w_gate.shape[-1]
    tm = EXPERT_ROWS
    n_blocks = P // tm
    xs = xs.reshape(P * tile[0], tile[1])

    def row_map(i, blk_e_ref, n_used_ref):
        return (jnp.minimum(i, n_used_ref[0] - 1), 0)

    def w_map(i, blk_e_ref, n_used_ref):
        return (layer, blk_e_ref[i], 0, 0)

    est = 2 * (2 * tm * D * 4 + 3 * D * F * 4) + 3 * D * F * 2 + tm * D * 2 + 3 * tm * F * 4
    grid_spec = pltpu.PrefetchScalarGridSpec(
        num_scalar_prefetch=2,
        grid=(n_blocks,),
        in_specs=[pl.BlockSpec((tm * tile[0], tile[1]), row_map),
                  pl.BlockSpec((1, 1, D, F), w_map),
                  pl.BlockSpec((1, 1, D, F), w_map),
                  pl.BlockSpec((1, 1, F, D), w_map)],
        out_specs=pl.BlockSpec((tm * tile[0], tile[1]), row_map),
        scratch_shapes=[pltpu.VMEM((D, F), BF16), pltpu.VMEM((D, F), BF16), pltpu.VMEM((F, D), BF16)],
    )
    out = pl.pallas_call(
        _expert_body,
        grid_spec=grid_spec,
        out_shape=jax.ShapeDtypeStruct(xs.shape, F32),
        compiler_params=pltpu.CompilerParams(dimension_semantics=("arbitrary",),
                                             vmem_limit_bytes=_vmem_limit(est)),
        name="expert_mlp",
    )(blk_e, n_used, xs, w_gate, w_up, w_down)
    return out.reshape((P,) + tile)


def _standardize(x, eps):
    mu = jnp.mean(x, -1, keepdims=True)
    var = jnp.mean(jnp.square(x - mu), -1, keepdims=True)
    return (x - mu) * lax.rsqrt(var + eps)


def _layer_norm(x, w, b):
    return _standardize(x, LN_EPS) * w + b


def _rms_norm(x, w):
    return x * lax.rsqrt(jnp.mean(x * x, -1, keepdims=True) + RMS_EPS) * w


def _l2_normalize(x):
    return x * lax.rsqrt(jnp.sum(x * x, -1, keepdims=True) + RMS_EPS)


def _rope_tables(n_ctx, n_lat):
    rows = n_lat // GRID_W
    row = jnp.repeat(jnp.arange(rows, dtype=F32), GRID_W)
    col = jnp.tile(jnp.arange(GRID_W, dtype=F32), rows)
    inv_freq = ROPE_THETA ** (-jnp.arange(0, ROPE_AXIS_DIM, 2, dtype=F32) / ROPE_AXIS_DIM)
    ang = jnp.stack([row[:, None] * inv_freq, col[:, None] * inv_freq], axis=1)
    cos = jnp.concatenate([jnp.ones((n_ctx,) + ang.shape[1:], F32), jnp.cos(ang)], axis=0)
    sin = jnp.concatenate([jnp.zeros((n_ctx,) + ang.shape[1:], F32), jnp.sin(ang)], axis=0)
    return cos, sin


def _apply_rope(x, cos, sin):
    B, Lt, H, hd = x.shape
    F = ROPE_AXIS_DIM // 2
    xf = x.reshape(B, Lt, H, 2, 2, F)
    x1, x2 = xf[..., 0, :], xf[..., 1, :]
    c, s = cos[None, :, None], sin[None, :, None]
    return jnp.stack([x1 * c - x2 * s, x2 * c + x1 * s], axis=-2).reshape(B, Lt, H, hd)


def _segment_conv(x, w, n_ctx):
    def conv(seg):
        n = seg.shape[1]
        pad = jnp.pad(seg, ((0, 0), (CONV_W // 2, CONV_W // 2), (0, 0)))
        return sum(pad[:, i:i + n] * w[i] for i in range(CONV_W))
    return jnp.concatenate([conv(x[:, :n_ctx]), conv(x[:, n_ctx:])], axis=1)


def _mixer(h_bf, w_main, w_ba, conv_w, a_log, dt_bias, gdn_norm_w, q_norm_w, k_norm_w, w_out_bf, cos, sin,
           *, B, Lt, n_ctx, skip_ctx_queries):
    T = B * Lt
    p = matmul(h_bf, w_main, tm=MM_ROWS, tn=1024)
    p_ba = matmul(h_bf, w_ba, tm=MM_ROWS, tn=V7X_LANES)
    p = p.reshape(B, Lt, -1)
    qkv = jax.nn.silu(_segment_conv(p[..., :OFF_Z], conv_w, n_ctx))
    heads = lambda t: t.reshape(B, Lt, GDN_HEADS, GDN_HEAD_DIM).transpose(0, 2, 1, 3)
    q, k, v = (heads(qkv[..., i * GDN_WIDTH:(i + 1) * GDN_WIDTH]) for i in range(3))
    q = _l2_normalize(q) * GDN_HEAD_DIM ** -0.5
    k = _l2_normalize(k)
    ba = p_ba[:, :4 * GDN_HEADS].reshape(B, Lt, 2, 2, GDN_HEADS)
    beta = jax.nn.sigmoid(ba[:, :, 0])
    g = -jnp.exp(a_log) * jax.nn.softplus(ba[:, :, 1] + dt_bias)
    gcol = jnp.concatenate([g, beta], axis=2).transpose(0, 3, 1, 2)
    grow = g.transpose(0, 3, 2, 1).reshape(B, GDN_HEADS, 2, Lt // CHUNK, CHUNK)
    o = gated_delta_bidirectional(q, k, v, gcol, grow, n_ctx_chunks=n_ctx // CHUNK)
    z = p[..., OFF_Z:OFF_BA].reshape(B, Lt, GDN_HEADS, GDN_HEAD_DIM)
    gdn = (_rms_norm(o.transpose(0, 2, 1, 3), gdn_norm_w) * jax.nn.silu(z)).reshape(B, Lt, GDN_WIDTH)
    a = p[..., OFF_BA:]
    qa = _rms_norm(a[..., :ATT_WIDTH].reshape(B, Lt, ATT_Q_HEADS, ATT_HEAD_DIM), q_norm_w)
    ka = _rms_norm(a[..., ATT_WIDTH:ATT_WIDTH + ATT_KV_WIDTH].reshape(B, Lt, ATT_KV_HEADS, ATT_HEAD_DIM), k_norm_w)
    va = a[..., ATT_WIDTH + ATT_KV_WIDTH:].reshape(B, Lt, ATT_KV_HEADS, ATT_HEAD_DIM)
    qa = _apply_rope(qa, cos, sin) * ATT_HEAD_DIM ** -0.5
    ka = _apply_rope(ka, cos, sin)
    qa = qa.astype(BF16).reshape(B, Lt, ATT_KV_HEADS, ATT_GROUP, ATT_HEAD_DIM).transpose(0, 2, 3, 1, 4)
    ka = ka.astype(BF16).transpose(0, 2, 1, 3)
    va = va.astype(BF16).transpose(0, 2, 1, 3)
    att = attention(qa, ka, va, n_ctx=n_ctx, skip_ctx_queries=skip_ctx_queries)
    if skip_ctx_queries:
        gdn = gdn[:, n_ctx:]
    mix = jnp.concatenate([gdn, att], axis=-1).astype(BF16)
    return matmul(mix.reshape(-1, mix.shape[-1]), w_out_bf, tm=MM_ROWS, tn=1024)


def _moe(h, layer, router_w_bf, router_bias, w_gate, w_up, w_down, sh_gate, sh_up, sh_down):
    T = h.shape[0]
    E = N_EXPERTS
    tm = EXPERT_ROWS
    eidx, wts, rank, counts = moe_router(h, router_w_bf, router_bias)
    counts = counts[0]
    padded = (counts + tm - 1) // tm * tm
    pad_ends = jnp.cumsum(padded)
    pad_starts = pad_ends - padded
    n_blocks = -(-T * TOP_K // tm) + E
    n_used = (pad_ends[-1] // tm).astype(jnp.int32)
    blk_first_row = jnp.minimum(jnp.arange(n_blocks, dtype=jnp.int32), n_used - 1) * tm
    blk_e = jnp.sum((pad_ends[None, :] <= blk_first_row[:, None]).astype(jnp.int32), axis=1)
    blk_e = jnp.minimum(blk_e, E - 1)
    eidx_flat, rank_flat = eidx.reshape(-1), rank.reshape(-1)
    xs = moe_dispatch(h, eidx_flat, rank_flat, pad_starts, pad_starts + counts, pad_ends, n_blocks * tm)
    out_rows = expert_mlp(xs, blk_e, n_used.reshape(1), layer, w_gate, w_up, w_down)
    n_sh = T // tm
    shared = expert_mlp(h, jnp.zeros((n_sh,), jnp.int32), jnp.full((1,), n_sh, jnp.int32), layer,
                        sh_gate[:, None], sh_up[:, None], sh_down[:, None])
    return moe_combine(out_rows, eidx_flat, rank_flat, pad_starts, wts.reshape(-1), shared)


def kernel(x, c, ctx, c_ctx, ada_w, ada_b, w_in, conv_w, gdn_a_log, gdn_dt_bias, gdn_norm_w, q_norm_w, k_norm_w,
           w_out, ln1_w, ln1_b, router_w, router_bias, exp_w_gate, exp_w_up, exp_w_down, sh_w_gate, sh_w_up,
           sh_w_down, ln2_w, ln2_b):
    B, L, D = x.shape
    Lc = ctx.shape[1]
    Lt = Lc + L
    depth = ada_w.shape[0]
    alpha = (2.0 * depth) ** 0.25
    cos, sin = _rope_tables(Lc, L)
    xa = _standardize(jnp.concatenate([ctx, x], axis=1), LN_EPS)
    is_ctx = (jnp.arange(Lt) < Lc)[None, :, None]
    cond = jnp.concatenate([c, c_ctx[None], jnp.zeros((16 - B - 1, D), F32)], axis=0)
    cond = jax.nn.silu(cond)
    for l in range(depth):
        last = l == depth - 1
        mod_all = matmul(cond, ada_w[l], tm=16, tn=1024) + ada_b[l]
        mod, mod_c = mod_all[:B], mod_all[B]
        pick = lambda i: jnp.where(is_ctx, mod_c[None, None, i * D:(i + 1) * D], mod[:, None, i * D:(i + 1) * D])
        sh1, sc1, g1, sh2, sc2, g2 = (pick(i) for i in range(6))
        w_l = w_in[l]
        w_main = jnp.concatenate([w_l[:, :OFF_BA], w_l[:, OFF_ATT:]], axis=1).astype(BF16)
        w_ba = jnp.pad(w_l[:, OFF_BA:OFF_ATT], ((0, 0), (0, V7X_LANES - 4 * GDN_HEADS))).astype(BF16)
        h = (xa * (1.0 + sc1) + sh1).astype(BF16).reshape(B * Lt, D)
        y = _mixer(h, w_main, w_ba, conv_w[l], gdn_a_log[l], gdn_dt_bias[l], gdn_norm_w[l], q_norm_w[l],
                   k_norm_w[l], w_out[l].astype(BF16), cos, sin, B=B, Lt=Lt, n_ctx=Lc, skip_ctx_queries=last)
        if last:
            xa, g1, sc2, sh2, g2 = (t[:, Lc:] for t in (xa, g1, sc2, sh2, g2))
        rows = xa.shape[1]
        xa = _layer_norm(alpha * xa + g1 * y.reshape(B, rows, D), ln1_w[l], ln1_b[l])
        h2 = (xa * (1.0 + sc2) + sh2).reshape(B * rows, V7X_SUBLANES, D // V7X_SUBLANES)
        ff = _moe(h2, l, router_w[l].astype(BF16), router_bias[l], exp_w_gate, exp_w_up, exp_w_down,
                  sh_w_gate, sh_w_up, sh_w_down)
        xa = _layer_norm(alpha * xa + g2 * ff.reshape(B, rows, D), ln2_w[l], ln2_b[l])
    return xa
```

```python
import functools

import jax
import jax.numpy as jnp
from jax import lax
from jax.experimental import pallas as pl
from jax.experimental.pallas import tpu as pltpu

F32 = jnp.float32
BF16 = jnp.bfloat16

GRID_W = 64
GDN_HEAD_DIM = 128
GDN_HEADS = 4
GDN_WIDTH = GDN_HEADS * GDN_HEAD_DIM
CONV_W = 5
CHUNK = 64
ATT_HEAD_DIM = 128
ATT_Q_HEADS = 4
ATT_KV_HEADS = 2
ATT_GROUP = ATT_Q_HEADS // ATT_KV_HEADS
ATT_WIDTH = ATT_Q_HEADS * ATT_HEAD_DIM
ATT_KV_WIDTH = ATT_KV_HEADS * ATT_HEAD_DIM
ROPE_AXIS_DIM = ATT_HEAD_DIM // 2
ROPE_THETA = 10000.0
OFF_Z = 3 * GDN_WIDTH
OFF_BA = 4 * GDN_WIDTH
OFF_ATT = OFF_BA + 4 * GDN_HEADS
N_EXPERTS = 256
TOP_K = 8
N_GROUPS = 8
TOPK_GROUPS = 4
EXPERT_DIM = 256
ROUTED_SCALE = 2.5
LN_EPS = 1e-5
RMS_EPS = 1e-6

V7X_LANES = 128
V7X_SUBLANES = 8
V7X_VMEM_BYTES = 64 * 1024 * 1024
VMEM_CAP = V7X_VMEM_BYTES - 8 * 1024 * 1024

EXPERT_ROWS = 256
GDN_CHUNKS_PER_ITER = 4
ROUTER_ROWS = 512
DISPATCH_TOKENS = 256
COMBINE_TOKENS = 128
DMA_TOKENS_PER_ITER = 2
ATT_Q_TILE = 256
MM_ROWS = 1024


def _vmem_limit(estimate_bytes):
    return int(min(VMEM_CAP, max(16 * 1024 * 1024, 2 * estimate_bytes)))


def _mm_body(a_ref, w_ref, o_ref):
    o_ref[...] = jnp.dot(a_ref[...].astype(BF16), w_ref[...].astype(BF16),
                         preferred_element_type=F32).astype(o_ref.dtype)


def matmul(a, w, *, tm, tn, out_dtype=F32):
    M, K = a.shape
    N = w.shape[1]
    assert M % tm == 0 and N % tn == 0, (M, N, tm, tn)
    est = 2 * (tm * K * a.dtype.itemsize + K * tn * w.dtype.itemsize + tm * tn * 4) + tm * K * 2 + K * tn * 2
    return pl.pallas_call(
        _mm_body,
        grid=(M // tm, N // tn),
        in_specs=[pl.BlockSpec((tm, K), lambda i, j: (i, 0)),
                  pl.BlockSpec((K, tn), lambda i, j: (0, j))],
        out_specs=pl.BlockSpec((tm, tn), lambda i, j: (i, j)),
        out_shape=jax.ShapeDtypeStruct((M, N), out_dtype),
        compiler_params=pltpu.CompilerParams(dimension_semantics=("parallel", "parallel"),
                                             vmem_limit_bytes=_vmem_limit(est)),
        name="proj_matmul",
    )(a, w)


def _unit_lower_inverses(a_list):
    n = a_list[0].shape[0]
    ii = lax.broadcasted_iota(jnp.int32, (n, n), 0)
    jj = lax.broadcasted_iota(jnp.int32, (n, n), 1)
    eye = jnp.where(ii == jj, 1.0, 0.0)
    mm = lambda a, b: jnp.dot(a.astype(BF16), b.astype(BF16), preferred_element_type=F32)
    pows = [-a for a in a_list]
    prods = [eye + p for p in pows]
    for _ in range(n.bit_length() - 2):
        pows = [mm(p, p) for p in pows]
        prods = [pr + mm(pr, p) for pr, p in zip(prods, pows)]
    return prods


def _gdn_body(q_ref, k_ref, v_ref, gcol_ref, grow_ref, o_ref,
              u_scr, wq_scr, kd_scr, qk_scr, cd_scr, *, n_chunks, n_ctx_chunks, chunks_per_iter):
    C = CHUNK
    dn_t = (((1,), (1,)), ((), ()))
    ii = lax.broadcasted_iota(jnp.int32, (C, C), 0)
    jj = lax.broadcasted_iota(jnp.int32, (C, C), 1)
    incl = (ii >= jj, ii <= jj)
    strict = (ii > jj, ii < jj)

    def pre_iter(it, carry):
        chunks = [it * chunks_per_iter + s for s in range(chunks_per_iter)]
        rows = [pl.multiple_of(c * C, C) for c in chunks]
        qs = [q_ref[0, 0, pl.ds(r0, C), :] for r0 in rows]
        ks = [k_ref[0, 0, pl.ds(r0, C), :] for r0 in rows]
        vs = [v_ref[0, 0, pl.ds(r0, C), :] for r0 in rows]
        kbs = [k.astype(BF16) for k in ks]
        kks = [lax.dot_general(kb, kb, dn_t, preferred_element_type=F32) for kb in kbs]
        qk_raws = [lax.dot_general(q.astype(BF16), kb, dn_t, preferred_element_type=F32) for q, kb in zip(qs, kbs)]
        combos = [(s, d) for s in range(chunks_per_iter) for d in (0, 1)]
        terms = []
        for s, d in combos:
            r0, c = rows[s], chunks[s]
            g_c = gcol_ref[0, 0, pl.ds(r0, C), d:d + 1]
            b_c = gcol_ref[0, 0, pl.ds(r0, C), 2 + d:3 + d]
            g_r = grow_ref[0, 0, d, pl.ds(c, 1), :]
            cum_col = jnp.sum(jnp.where(incl[d], g_r, 0.0), axis=1, keepdims=True)
            cum_row = jnp.sum(jnp.where(incl[1 - d], g_c, 0.0), axis=0, keepdims=True)
            cum_last = jnp.sum(g_r, axis=1, keepdims=True)
            decay = jnp.where(incl[d], jnp.exp(jnp.where(incl[d], cum_col - cum_row, 0.0)), 0.0)
            a_mat = jnp.where(strict[d], b_c * kks[s] * decay, 0.0)
            terms.append((b_c, cum_col, cum_last, decay, a_mat))
        t_invs = _unit_lower_inverses([t[4] for t in terms])
        sols = []
        for (s, d), (b_c, cum_col, cum_last, decay, _), t_inv in zip(combos, terms, t_invs):
            e_g = jnp.exp(cum_col)
            rhs = jnp.concatenate([vs[s] * b_c, ks[s] * (b_c * e_g)], axis=1)
            sols.append((e_g, jnp.dot(t_inv.astype(BF16), rhs.astype(BF16), preferred_element_type=F32)))
        for (s, d), (b_c, cum_col, cum_last, decay, _), (e_g, sol) in zip(combos, terms, sols):
            r0, c = rows[s], chunks[s]
            r1 = pl.multiple_of(c * 2 * C, 2 * C)
            u_scr[d, pl.ds(r0, C), :] = sol[:, :GDN_HEAD_DIM]
            wq_scr[d, pl.ds(r1, C), :] = sol[:, GDN_HEAD_DIM:].astype(BF16)
            wq_scr[d, pl.ds(r1 + C, C), :] = (qs[s] * e_g).astype(BF16)
            kd_scr[d, pl.ds(r0, C), :] = (ks[s] * jnp.exp(cum_last - cum_col)).astype(BF16)
            qk_scr[d, pl.ds(r0, C), :] = (qk_raws[s] * decay).astype(BF16)
            cd_scr[d, pl.ds(c, 1), :] = jnp.broadcast_to(jnp.exp(cum_last), (1, GDN_HEAD_DIM))
        return carry

    lax.fori_loop(0, n_chunks // chunks_per_iter, pre_iter, 0)

    o_ref[...] = jnp.zeros_like(o_ref)

    def step(t, states):
        c_bwd = jnp.where(t < n_ctx_chunks, n_ctx_chunks - 1 - t, n_chunks - 1 + n_ctx_chunks - t)
        chunks = (t, c_bwd)
        rows = [pl.multiple_of(c * C, C) for c in chunks]
        ws_qs = [jnp.dot(wq_scr[d, pl.ds(pl.multiple_of(chunks[d] * 2 * C, 2 * C), 2 * C), :],
                         states[d].astype(BF16), preferred_element_type=F32) for d in (0, 1)]
        v_new_b = [(u_scr[d, pl.ds(rows[d], C), :] - ws_qs[d][:C]).astype(BF16) for d in (0, 1)]
        intra = [jnp.dot(qk_scr[d, pl.ds(rows[d], C), :], v_new_b[d], preferred_element_type=F32) for d in (0, 1)]
        upd = [lax.dot_general(kd_scr[d, pl.ds(rows[d], C), :], v_new_b[d], (((0,), (0,)), ((), ())),
                               preferred_element_type=F32) for d in (0, 1)]
        for d in (0, 1):
            o_ref[0, 0, pl.ds(rows[d], C), :] += ws_qs[d][C:] + intra[d]
        return tuple(states[d] * cd_scr[d, pl.ds(chunks[d], 1), :] + upd[d] for d in (0, 1))

    s0 = jnp.zeros((GDN_HEAD_DIM, GDN_HEAD_DIM), F32)
    lax.fori_loop(0, n_chunks, step, (s0, s0))


def gated_delta_bidirectional(q, k, v, gcol, grow, *, n_ctx_chunks):
    B, H, Lt, dk = q.shape
    n_chunks = Lt // CHUNK
    chunks_per_iter = GDN_CHUNKS_PER_ITER
    assert n_chunks % chunks_per_iter == 0
    cd_rows = -(-n_chunks // 8) * 8
    seq_spec = pl.BlockSpec((1, 1, Lt, dk), lambda b, h: (b, h, 0, 0))
    scratch = [pltpu.VMEM((2, Lt, dk), F32),
               pltpu.VMEM((2, 2 * Lt, dk), BF16),
               pltpu.VMEM((2, Lt, dk), BF16),
               pltpu.VMEM((2, Lt, CHUNK), BF16),
               pltpu.VMEM((2, cd_rows, dk), F32)]
    est = (2 * 4 * Lt * dk * 4
           + 2 * Lt * V7X_LANES * 4
           + 2 * Lt * dk * 4 + 2 * 2 * Lt * dk * 2 + 2 * Lt * dk * 2 + 2 * Lt * V7X_LANES * 2)
    body = functools.partial(_gdn_body, n_chunks=n_chunks, n_ctx_chunks=n_ctx_chunks,
                             chunks_per_iter=chunks_per_iter)
    return pl.pallas_call(
        body,
        grid=(B, H),
        in_specs=[seq_spec, seq_spec, seq_spec,
                  pl.BlockSpec((1, 1, Lt, 4), lambda b, h: (b, h, 0, 0)),
                  pl.BlockSpec((1, 1, 2, n_chunks, CHUNK), lambda b, h: (b, h, 0, 0, 0))],
        out_specs=seq_spec,
        out_shape=jax.ShapeDtypeStruct((B, H, Lt, dk), F32),
        scratch_shapes=scratch,
        compiler_params=pltpu.CompilerParams(dimension_semantics=("parallel", "parallel"),
                                             vmem_limit_bytes=_vmem_limit(est // 2 + est // 4)),
        name="gated_delta_scan",
    )(q, k, v, gcol, grow)


def _attn_body(q_ref, k_ref, v_ref, o_ref, *, n_ctx, first_tile):
    G, tq, hd = q_ref.shape[2:]
    tile = pl.program_id(2) + first_tile
    q = q_ref[0, 0].reshape(G * tq, hd)
    s = lax.dot_general(q, k_ref[0, 0], (((1,), (1,)), ((), ())), preferred_element_type=F32)
    col = lax.broadcasted_iota(jnp.int32, s.shape, 1)
    allowed = jnp.logical_or(tile * tq >= n_ctx, col < n_ctx)
    s = jnp.where(allowed, s, -1e30)
    m = jnp.max(s, axis=1, keepdims=True)
    p = jnp.exp(s - m)
    denom = jnp.sum(p, axis=1, keepdims=True)
    o = jnp.dot(p.astype(BF16), v_ref[0, 0], preferred_element_type=F32) / denom
    for g in range(G):
        o_ref[0, :, g * hd:(g + 1) * hd] = o[g * tq:(g + 1) * tq]


def attention(q, k, v, *, n_ctx, skip_ctx_queries):
    B, Hkv, G, Lt, hd = q.shape
    tq = ATT_Q_TILE
    assert Lt % tq == 0 and n_ctx % tq == 0
    first_tile = n_ctx // tq if skip_ctx_queries else 0
    n_tiles = Lt // tq - first_tile
    est = 2 * (G * tq * hd * 2 + 2 * Lt * hd * 2 + tq * G * hd * 4) + 3 * G * tq * Lt * 4
    body = functools.partial(_attn_body, n_ctx=n_ctx, first_tile=first_tile)
    return pl.pallas_call(
        body,
        grid=(B, Hkv, n_tiles),
        in_specs=[pl.BlockSpec((1, 1, G, tq, hd), lambda b, h, i: (b, h, 0, i + first_tile, 0)),
                  pl.BlockSpec((1, 1, Lt, hd), lambda b, h, i: (b, h, 0, 0)),
                  pl.BlockSpec((1, 1, Lt, hd), lambda b, h, i: (b, h, 0, 0))],
        out_specs=pl.BlockSpec((1, tq, G * hd), lambda b, h, i: (b, i, h)),
        out_shape=jax.ShapeDtypeStruct((B, n_tiles * tq, Hkv * G * hd), F32),
        compiler_params=pltpu.CompilerParams(dimension_semantics=("parallel", "parallel", "parallel"),
                                             vmem_limit_bytes=_vmem_limit(est // 2)),
        name="gqa_attention",
    )(q, k, v)


def _rows_2d(x_ref):
    tm = x_ref.shape[0] // V7X_SUBLANES
    return jnp.concatenate([x_ref[pl.ds(c, tm, stride=V7X_SUBLANES), :] for c in range(V7X_SUBLANES)], axis=1)


def _router_body(h_ref, w_ref, bias_ref, eidx_ref, wts_ref, rank_ref, counts_ref, carry_scr):
    tm = h_ref.shape[0] // V7X_SUBLANES
    E = N_EXPERTS
    per_group = E // N_GROUPS
    neg = -jnp.inf

    @pl.when(pl.program_id(0) == 0)
    def _():
        carry_scr[...] = jnp.zeros_like(carry_scr)

    logits = jnp.dot(_rows_2d(h_ref).astype(BF16), w_ref[...], preferred_element_type=F32)
    s = jax.nn.sigmoid(logits)
    sel = s + bias_ref[...]
    lane = lax.broadcasted_iota(jnp.int32, (tm, E), 1)
    lane_f = lane.astype(F32)
    grp = lane // per_group
    scores = []
    for g in range(N_GROUPS):
        v = jnp.where(grp == g, sel, neg)
        m1 = jnp.max(v, axis=1, keepdims=True)
        i1 = jnp.min(jnp.where(v == m1, lane_f, float(E)), axis=1, keepdims=True)
        m2 = jnp.max(jnp.where(lane_f == i1, neg, v), axis=1, keepdims=True)
        scores.append(m1 + m2)
    allowed = jnp.zeros((tm, E), jnp.bool_)
    for g in range(N_GROUPS):
        beaten = jnp.zeros((tm, 1), F32)
        for o in range(N_GROUPS):
            if o != g:
                wins = (scores[o] > scores[g]) if o > g else (scores[o] >= scores[g])
                beaten = beaten + wins.astype(F32)
        allowed = jnp.logical_or(allowed, jnp.logical_and(grp == g, beaten < TOPK_GROUPS))
    masked = jnp.where(allowed, sel, neg)
    idx_cols, s_cols = [], []
    chosen = jnp.zeros((tm, E), jnp.bool_)
    for _ in range(TOP_K):
        m = jnp.max(masked, axis=1, keepdims=True)
        idx = jnp.min(jnp.where(masked == m, lane_f, float(E)), axis=1, keepdims=True)
        hit = lane_f == idx
        s_cols.append(jnp.sum(jnp.where(hit, s, 0.0), axis=1, keepdims=True))
        idx_cols.append(idx)
        chosen = jnp.logical_or(chosen, hit)
        masked = jnp.where(hit, neg, masked)
    ri = lax.broadcasted_iota(jnp.int32, (tm, tm), 0)
    ci = lax.broadcasted_iota(jnp.int32, (tm, tm), 1)
    running = jnp.dot((ci < ri).astype(BF16), chosen.astype(BF16), preferred_element_type=F32) + carry_scr[...]
    total = s_cols[0]
    for col in s_cols[1:]:
        total = total + col
    k_lane = lax.broadcasted_iota(jnp.int32, (tm, TOP_K), 1)
    eidx = jnp.zeros((tm, TOP_K), F32)
    wts = jnp.zeros((tm, TOP_K), F32)
    rank = jnp.zeros((tm, TOP_K), F32)
    for j in range(TOP_K):
        r_j = jnp.sum(jnp.where(lane_f == idx_cols[j], running, 0.0), axis=1, keepdims=True)
        eidx = jnp.where(k_lane == j, idx_cols[j], eidx)
        wts = jnp.where(k_lane == j, s_cols[j] / total * ROUTED_SCALE, wts)
        rank = jnp.where(k_lane == j, r_j, rank)
    eidx_ref[...] = eidx.astype(jnp.int32)
    wts_ref[...] = wts
    rank_ref[...] = rank.astype(jnp.int32)
    carry_scr[...] = carry_scr[...] + jnp.sum(chosen.astype(F32), axis=0, keepdims=True)
    counts_ref[...] = carry_scr[...].astype(jnp.int32)


def moe_router(h, router_w_bf, router_bias):
    T = h.shape[0]
    D, E = router_w_bf.shape
    tm = ROUTER_ROWS
    assert T % tm == 0
    tok = pl.BlockSpec((tm, TOP_K), lambda i: (i, 0))
    est = 2 * (tm * D * 4 + D * E * 2) + tm * tm * 4 + 24 * tm * E * 4
    h = h.reshape(T * V7X_SUBLANES, V7X_LANES)
    return pl.pallas_call(
        _router_body,
        grid=(T // tm,),
        in_specs=[pl.BlockSpec((tm * V7X_SUBLANES, V7X_LANES), lambda i: (i, 0)),
                  pl.BlockSpec((D, E), lambda i: (0, 0)),
                  pl.BlockSpec((1, E), lambda i: (0, 0))],
        out_specs=[tok, tok, tok, pl.BlockSpec((1, E), lambda i: (0, 0))],
        out_shape=[jax.ShapeDtypeStruct((T, TOP_K), jnp.int32), jax.ShapeDtypeStruct((T, TOP_K), F32),
                   jax.ShapeDtypeStruct((T, TOP_K), jnp.int32), jax.ShapeDtypeStruct((1, E), jnp.int32)],
        scratch_shapes=[pltpu.VMEM((1, E), F32)],
        compiler_params=pltpu.CompilerParams(dimension_semantics=("arbitrary",),
                                             vmem_limit_bytes=_vmem_limit(est // 2)),
        name="moe_router",
    )(h, router_w_bf, router_bias.reshape(1, E))


def _dispatch_body(pstart_ref, pad_lo_ref, pad_hi_ref, eidx_ref, rank_ref, h_ref, xs_hbm, zero_scr, sems):
    i = pl.program_id(0)
    tt = h_ref.shape[0]

    def row_copy(src, dst_row, sem):
        return pltpu.make_async_copy(src, xs_hbm.at[dst_row], sem)

    @pl.when(i == 0)
    def _():
        zero_scr[...] = jnp.zeros_like(zero_scr)
        for wait in (False, True):
            def per_expert(e, carry, wait=wait):
                def per_row(r, c):
                    cp = row_copy(zero_scr, r, sems.at[1])
                    cp.wait() if wait else cp.start()
                    return c
                return lax.fori_loop(pad_lo_ref[e], pad_hi_ref[e], per_row, carry)
            lax.fori_loop(0, N_EXPERTS, per_expert, 0)

    def issue(p, c):
        toks = [p * DMA_TOKENS_PER_ITER + u for u in range(DMA_TOKENS_PER_ITER)]
        dst = [[pstart_ref[eidx_ref[r * TOP_K + j]] + rank_ref[r * TOP_K + j] for j in range(TOP_K)]
               for r in toks]
        for r, dst_r in zip(toks, dst):
            for d in dst_r:
                row_copy(h_ref.at[r], d, sems.at[0]).start()
        return c
    lax.fori_loop(0, tt // DMA_TOKENS_PER_ITER, issue, 0)

    def drain(r, c):
        for j in range(TOP_K):
            row_copy(h_ref.at[0], 0, sems.at[0]).wait()
        return c
    lax.fori_loop(0, tt, drain, 0)


def moe_dispatch(h, eidx_flat, rank_flat, pad_starts, pad_lo, pad_hi, n_rows):
    T = h.shape[0]
    tile = h.shape[1:]
    tt = DISPATCH_TOKENS
    assert T % tt == 0
    idx_spec = pl.BlockSpec((tt * TOP_K,), lambda i, *_: (i,), memory_space=pltpu.SMEM)
    grid_spec = pltpu.PrefetchScalarGridSpec(
        num_scalar_prefetch=3, grid=(T // tt,),
        in_specs=[idx_spec, idx_spec, pl.BlockSpec((tt,) + tile, lambda i, *_: (i, 0, 0))],
        out_specs=pl.BlockSpec(memory_space=pl.ANY),
        scratch_shapes=[pltpu.VMEM(tile, h.dtype), pltpu.SemaphoreType.DMA((2,))])
    return pl.pallas_call(
        _dispatch_body,
        grid_spec=grid_spec,
        out_shape=jax.ShapeDtypeStruct((n_rows,) + tile, h.dtype),
        compiler_params=pltpu.CompilerParams(dimension_semantics=("arbitrary",)),
        name="moe_dispatch",
    )(pad_starts, pad_lo, pad_hi, eidx_flat, rank_flat, h)


def _combine_body(pstart_ref, eidx_ref, rank_ref, wts_ref, shared_ref, rows_hbm, o_ref, buf, sems, *, tt, n_tiles):
    i = pl.program_id(0)

    def row_copy(src_row, slot, j, r):
        return pltpu.make_async_copy(rows_hbm.at[src_row], buf.at[slot, j, r], sems.at[slot])

    @pl.when(i < n_tiles)
    def _():
        slot = i % 2

        def issue(p, c):
            toks = [p * DMA_TOKENS_PER_ITER + u for u in range(DMA_TOKENS_PER_ITER)]
            src = [[pstart_ref[eidx_ref[r * TOP_K + j]] + rank_ref[r * TOP_K + j] for j in range(TOP_K)]
                   for r in toks]
            for r, src_r in zip(toks, src):
                for j, s in enumerate(src_r):
                    row_copy(s, slot, j, r).start()
            return c
        lax.fori_loop(0, tt // DMA_TOKENS_PER_ITER, issue, 0)

    @pl.when(i >= 1)
    def _():
        slot = (i + 1) % 2

        def drain(r, c):
            for j in range(TOP_K):
                row_copy(0, slot, j, 0).wait()
            return c
        lax.fori_loop(0, tt, drain, 0)

        def per_token(r, c):
            acc = shared_ref[r]
            for j in range(TOP_K):
                acc = acc + buf[slot, j, r] * wts_ref[r * TOP_K + j]
            o_ref[r] = acc
            return c
        lax.fori_loop(0, tt, per_token, 0, unroll=4)


def moe_combine(rows, eidx_flat, rank_flat, pad_starts, wts_flat, shared):
    T = shared.shape[0]
    tile = shared.shape[1:]
    tt = COMBINE_TOKENS
    assert T % tt == 0
    n_tiles = T // tt
    prev = lambda i, *_: (jnp.maximum(i - 1, 0), 0, 0)
    idx_spec = pl.BlockSpec((tt * TOP_K,), lambda i, *_: (jnp.minimum(i, n_tiles - 1),), memory_space=pltpu.SMEM)
    wts_spec = pl.BlockSpec((tt * TOP_K,), lambda i, *_: (jnp.maximum(i - 1, 0),), memory_space=pltpu.SMEM)
    grid_spec = pltpu.PrefetchScalarGridSpec(
        num_scalar_prefetch=1, grid=(n_tiles + 1,),
        in_specs=[idx_spec, idx_spec, wts_spec, pl.BlockSpec((tt,) + tile, prev),
                  pl.BlockSpec(memory_space=pl.ANY)],
        out_specs=pl.BlockSpec((tt,) + tile, prev),
        scratch_shapes=[pltpu.VMEM((2, TOP_K, tt) + tile, F32), pltpu.SemaphoreType.DMA((2,))])
    row_bytes = tile[0] * tile[1] * 4
    est = 2 * TOP_K * tt * row_bytes + 4 * tt * row_bytes
    return pl.pallas_call(
        functools.partial(_combine_body, tt=tt, n_tiles=n_tiles),
        grid_spec=grid_spec,
        out_shape=jax.ShapeDtypeStruct((T,) + tile, F32),
        compiler_params=pltpu.CompilerParams(dimension_semantics=("arbitrary",),
                                             vmem_limit_bytes=_vmem_limit(est)),
        name="moe_combine",
    )(pad_starts, eidx_flat, rank_flat, wts_flat, shared, rows)


def _expert_body(blk_e_ref, n_used_ref, x_ref, wg_ref, wu_ref, wd_ref, o_ref, wg_b, wu_b, wd_b):
    i = pl.program_id(0)

    @pl.when(jnp.logical_or(i == 0, blk_e_ref[i] != blk_e_ref[jnp.maximum(i - 1, 0)]))
    def _():
        wg_b[...] = wg_ref[0, 0].astype(BF16)
        wu_b[...] = wu_ref[0, 0].astype(BF16)
        wd_b[...] = wd_ref[0, 0].astype(BF16)

    @pl.when(i < n_used_ref[0])
    def _():
        x = _rows_2d(x_ref).astype(BF16)
        a = jnp.dot(x, wg_b[...], preferred_element_type=F32)
        b = jnp.dot(x, wu_b[...], preferred_element_type=F32)
        h = (a * jax.nn.sigmoid(a)) * b
        o = jnp.dot(h.astype(BF16), wd_b[...], preferred_element_type=F32)
        tm = o.shape[0]
        for c in range(V7X_SUBLANES):
            o_ref[pl.ds(c, tm, stride=V7X_SUBLANES), :] = o[:, c * V7X_LANES:(c + 1) * V7X_LANES]


def expert_mlp(xs, blk_e, n_used, layer, w_gate, w_up, w_down):
    P = xs.shape[0]
    tile = xs.shape[1:]
    D = tile[0] * tile[1]
    F = w_gate.shape[-1]
    tm = EXPERT_ROWS
    n_blocks = P // tm
    xs = xs.reshape(P * tile[0], tile[1])

    def row_map(i, blk_e_ref, n_used_ref):
        return (jnp.minimum(i, n_used_ref[0] - 1), 0)

    def w_map(i, blk_e_ref, n_used_ref):
        return (layer, blk_e_ref[i], 0, 0)

    est = 2 * (2 * tm * D * 4 + 3 * D * F * 4) + 3 * D * F * 2 + tm * D * 2 + 3 * tm * F * 4
    grid_spec = pltpu.PrefetchScalarGridSpec(
        num_scalar_prefetch=2,
        grid=(n_blocks,),
        in_specs=[pl.BlockSpec((tm * tile[0], tile[1]), row_map),
                  pl.BlockSpec((1, 1, D, F), w_map),
                  pl.BlockSpec((1, 1, D, F), w_map),
                  pl.BlockSpec((1, 1, F, D), w_map)],
        out_specs=pl.BlockSpec((tm * tile[0], tile[1]), row_map),
        scratch_shapes=[pltpu.VMEM((D, F), BF16), pltpu.VMEM((D, F), BF16), pltpu.VMEM((F, D), BF16)],
    )
    out = pl.pallas_call(
        _expert_body,
        grid_spec=grid_spec,
        out_shape=jax.ShapeDtypeStruct(xs.shape, F32),
        compiler_params=pltpu.CompilerParams(dimension_semantics=("arbitrary",),
                                             vmem_limit_bytes=_vmem_limit(est)),
        name="expert_mlp",
    )(blk_e, n_used, xs, w_gate, w_up, w_down)
    return out.reshape((P,) + tile)


def _standardize(x, eps):
    mu = jnp.mean(x, -1, keepdims=True)
    var = jnp.mean(jnp.square(x - mu), -1, keepdims=True)
    return (x - mu) * lax.rsqrt(var + eps)


def _layer_norm(x, w, b):
    return _standardize(x, LN_EPS) * w + b


def _rms_norm(x, w):
    return x * lax.rsqrt(jnp.mean(x * x, -1, keepdims=True) + RMS_EPS) * w


def _l2_normalize(x):
    return x * lax.rsqrt(jnp.sum(x * x, -1, keepdims=True) + RMS_EPS)


def _rope_tables(n_ctx, n_lat):
    rows = n_lat // GRID_W
    row = jnp.repeat(jnp.arange(rows, dtype=F32), GRID_W)
    col = jnp.tile(jnp.arange(GRID_W, dtype=F32), rows)
    inv_freq = ROPE_THETA ** (-jnp.arange(0, ROPE_AXIS_DIM, 2, dtype=F32) / ROPE_AXIS_DIM)
    ang = jnp.stack([row[:, None] * inv_freq, col[:, None] * inv_freq], axis=1)
    cos = jnp.concatenate([jnp.ones((n_ctx,) + ang.shape[1:], F32), jnp.cos(ang)], axis=0)
    sin = jnp.concatenate([jnp.zeros((n_ctx,) + ang.shape[1:], F32), jnp.sin(ang)], axis=0)
    return cos, sin


def _apply_rope(x, cos, sin):
    B, Lt, H, hd = x.shape
    F = ROPE_AXIS_DIM // 2
    xf = x.reshape(B, Lt, H, 2, 2, F)
    x1, x2 = xf[..., 0, :], xf[..., 1, :]
    c, s = cos[None, :, None], sin[None, :, None]
    return jnp.stack([x1 * c - x2 * s, x2 * c + x1 * s], axis=-2).reshape(B, Lt, H, hd)


def _segment_conv(x, w, n_ctx):
    def conv(seg):
        n = seg.shape[1]
        pad = jnp.pad(seg, ((0, 0), (CONV_W // 2, CONV_W // 2), (0, 0)))
        return sum(pad[:, i:i + n] * w[i] for i in range(CONV_W))
    return jnp.concatenate([conv(x[:, :n_ctx]), conv(x[:, n_ctx:])], axis=1)


def _mixer(h_bf, w_main, w_ba, conv_w, a_log, dt_bias, gdn_norm_w, q_norm_w, k_norm_w, w_out_bf, cos, sin,
           *, B, Lt, n_ctx, skip_ctx_queries):
    T = B * Lt
    p = matmul(h_bf, w_main, tm=MM_ROWS, tn=1024)
    p_ba = matmul(h_bf, w_ba, tm=MM_ROWS, tn=V7X_LANES)
    p = p.reshape(B, Lt, -1)
    qkv = jax.nn.silu(_segment_conv(p[..., :OFF_Z], conv_w, n_ctx))
    heads = lambda t: t.reshape(B, Lt, GDN_HEADS, GDN_HEAD_DIM).transpose(0, 2, 1, 3)
    q, k, v = (heads(qkv[..., i * GDN_WIDTH:(i + 1) * GDN_WIDTH]) for i in range(3))
    q = _l2_normalize(q) * GDN_HEAD_DIM ** -0.5
    k = _l2_normalize(k)
    ba = p_ba[:, :4 * GDN_HEADS].reshape(B, Lt, 2, 2, GDN_HEADS)
    beta = jax.nn.sigmoid(ba[:, :, 0])
    g = -jnp.exp(a_log) * jax.nn.softplus(ba[:, :, 1] + dt_bias)
    gcol = jnp.concatenate([g, beta], axis=2).transpose(0, 3, 1, 2)
    grow = g.transpose(0, 3, 2, 1).reshape(B, GDN_HEADS, 2, Lt // CHUNK, CHUNK)
    o = gated_delta_bidirectional(q, k, v, gcol, grow, n_ctx_chunks=n_ctx // CHUNK)
    z = p[..., OFF_Z:OFF_BA].reshape(B, Lt, GDN_HEADS, GDN_HEAD_DIM)
    gdn = (_rms_norm(o.transpose(0, 2, 1, 3), gdn_norm_w) * jax.nn.silu(z)).reshape(B, Lt, GDN_WIDTH)
    a = p[..., OFF_BA:]
    qa = _rms_norm(a[..., :ATT_WIDTH].reshape(B, Lt, ATT_Q_HEADS, ATT_HEAD_DIM), q_norm_w)
    ka = _rms_norm(a[..., ATT_WIDTH:ATT_WIDTH + ATT_KV_WIDTH].reshape(B, Lt, ATT_KV_HEADS, ATT_HEAD_DIM), k_norm_w)
    va = a[..., ATT_WIDTH + ATT_KV_WIDTH:].reshape(B, Lt, ATT_KV_HEADS, ATT_HEAD_DIM)
    qa = _apply_rope(qa, cos, sin) * ATT_HEAD_DIM ** -0.5
    ka = _apply_rope(ka, cos, sin)
    qa = qa.astype(BF16).reshape(B, Lt, ATT_KV_HEADS, ATT_GROUP, ATT_HEAD_DIM).transpose(0, 2, 3, 1, 4)
    ka = ka.astype(BF16).transpose(0, 2, 1, 3)
    va = va.astype(BF16).transpose(0, 2, 1, 3)
    att = attention(qa, ka, va, n_ctx=n_ctx, skip_ctx_queries=skip_ctx_queries)
    if skip_ctx_queries:
        gdn = gdn[:, n_ctx:]
    mix = jnp.concatenate([gdn, att], axis=-1).astype(BF16)
    return matmul(mix.reshape(-1, mix.shape[-1]), w_out_bf, tm=MM_ROWS, tn=1024)


def _moe(h, layer, router_w_bf, router_bias, w_gate, w_up, w_down, sh_gate, sh_up, sh_down):
    T = h.shape[0]
    E = N_EXPERTS
    tm = EXPERT_ROWS
    eidx, wts, rank, counts = moe_router(h, router_w_bf, router_bias)
    counts = counts[0]
    padded = (counts + tm - 1) // tm * tm
    pad_ends = jnp.cumsum(padded)
    pad_starts = pad_ends - padded
    n_blocks = -(-T * TOP_K // tm) + E
    n_used = (pad_ends[-1] // tm).astype(jnp.int32)
    blk_first_row = jnp.minimum(jnp.arange(n_blocks, dtype=jnp.int32), n_used - 1) * tm
    blk_e = jnp.sum((pad_ends[None, :] <= blk_first_row[:, None]).astype(jnp.int32), axis=1)
    blk_e = jnp.minimum(blk_e, E - 1)
    eidx_flat, rank_flat = eidx.reshape(-1), rank.reshape(-1)
    xs = moe_dispatch(h, eidx_flat, rank_flat, pad_starts, pad_starts + counts, pad_ends, n_blocks * tm)
    out_rows = expert_mlp(xs, blk_e, n_used.reshape(1), layer, w_gate, w_up, w_down)
    n_sh = T // tm
    shared = expert_mlp(h, jnp.zeros((n_sh,), jnp.int32), jnp.full((1,), n_sh, jnp.int32), layer,
                        sh_gate[:, None], sh_up[:, None], sh_down[:, None])
    return moe_combine(out_rows, eidx_flat, rank_flat, pad_starts, wts.reshape(-1), shared)


def kernel(x, c, ctx, c_ctx, ada_w, ada_b, w_in, conv_w, gdn_a_log, gdn_dt_bias, gdn_norm_w, q_norm_w, k_norm_w,
           w_out, ln1_w, ln1_b, router_w, router_bias, exp_w_gate, exp_w_up, exp_w_down, sh_w_gate, sh_w_up,
           sh_w_down, ln2_w, ln2_b):
    B, L, D = x.shape
    Lc = ctx.shape[1]
    Lt = Lc + L
    depth = ada_w.shape[0]
    alpha = (2.0 * depth) ** 0.25
    cos, sin = _rope_tables(Lc, L)
    xa = _standardize(jnp.concatenate([ctx, x], axis=1), LN_EPS)
    is_ctx = (jnp.arange(Lt) < Lc)[None, :, None]
    cond = jnp.concatenate([c, c_ctx[None], jnp.zeros((16 - B - 1, D), F32)], axis=0)
    cond = jax.nn.silu(cond)
    for l in range(depth):
        last = l == depth - 1
        mod_all = matmul(cond, ada_w[l], tm=16, tn=1024) + ada_b[l]
        mod, mod_c = mod_all[:B], mod_all[B]
        pick = lambda i: jnp.where(is_ctx, mod_c[None, None, i * D:(i + 1) * D], mod[:, None, i * D:(i + 1) * D])
        sh1, sc1, g1, sh2, sc2, g2 = (pick(i) for i in range(6))
        w_l = w_in[l]
        w_main = jnp.concatenate([w_l[:, :OFF_BA], w_l[:, OFF_ATT:]], axis=1).astype(BF16)
        w_ba = jnp.pad(w_l[:, OFF_BA:OFF_ATT], ((0, 0), (0, V7X_LANES - 4 * GDN_HEADS))).astype(BF16)
        h = (xa * (1.0 + sc1) + sh1).astype(BF16).reshape(B * Lt, D)
        y = _mixer(h, w_main, w_ba, conv_w[l], gdn_a_log[l], gdn_dt_bias[l], gdn_norm_w[l], q_norm_w[l],
                   k_norm_w[l], w_out[l].astype(BF16), cos, sin, B=B, Lt=Lt, n_ctx=Lc, skip_ctx_queries=last)
        if last:
            xa, g1, sc2, sh2, g2 = (t[:, Lc:] for t in (xa, g1, sc2, sh2, g2))
        rows = xa.shape[1]
        xa = _layer_norm(alpha * xa + g1 * y.reshape(B, rows, D), ln1_w[l], ln1_b[l])
        h2 = (xa * (1.0 + sc2) + sh2).reshape(B * rows, V7X_SUBLANES, D // V7X_SUBLANES)
        ff = _moe(h2, l, router_w[l].astype(BF16), router_bias[l], exp_w_gate, exp_w_up, exp_w_down,
                  sh_w_gate, sh_w_up, sh_w_down)
        xa = _layer_norm(alpha * xa + g2 * ff.reshape(B, rows, D), ln2_w[l], ln2_b[l])
    return xa
```

```python
import functools

import jax
import jax.numpy as jnp
from jax import lax
from jax.experimental import pallas as pl
from jax.experimental.pallas import tpu as pltpu

F32 = jnp.float32
BF16 = jnp.bfloat16

GRID_W = 64
GDN_HEAD_DIM = 128
GDN_HEADS = 4
GDN_WIDTH = GDN_HEADS * GDN_HEAD_DIM
CONV_W = 5
CHUNK = 64
ATT_HEAD_DIM = 128
ATT_Q_HEADS = 4
ATT_KV_HEADS = 2
ATT_GROUP = ATT_Q_HEADS // ATT_KV_HEADS
ATT_WIDTH = ATT_Q_HEADS * ATT_HEAD_DIM
ATT_KV_WIDTH = ATT_KV_HEADS * ATT_HEAD_DIM
ROPE_AXIS_DIM = ATT_HEAD_DIM // 2
ROPE_THETA = 10000.0
OFF_Z = 3 * GDN_WIDTH
OFF_BA = 4 * GDN_WIDTH
OFF_ATT = OFF_BA + 4 * GDN_HEADS
N_EXPERTS = 256
TOP_K = 8
N_GROUPS = 8
TOPK_GROUPS = 4
EXPERT_DIM = 256
ROUTED_SCALE = 2.5
LN_EPS = 1e-5
RMS_EPS = 1e-6

V7X_LANES = 128
V7X_SUBLANES = 8
V7X_VMEM_BYTES = 64 * 1024 * 1024
VMEM_CAP = V7X_VMEM_BYTES - 8 * 1024 * 1024

EXPERT_ROWS = 256
GDN_CHUNKS_PER_ITER = 4
GDN_PREP_ROWS = 256
ROUTER_ROWS = 512
DISPATCH_TOKENS = 256
COMBINE_TOKENS = 128
DMA_TOKENS_PER_ITER = 2
ATT_Q_TILE = 256
ATT_KV_TILE = 256
MM_ROWS = 1024


def _vmem_limit(estimate_bytes):
    return int(min(VMEM_CAP, max(16 * 1024 * 1024, 2 * estimate_bytes)))


def _mm_body(a_ref, w_ref, o_ref):
    o_ref[...] = jnp.dot(a_ref[...].astype(BF16), w_ref[...].astype(BF16),
                         preferred_element_type=F32).astype(o_ref.dtype)


def matmul(a, w, *, tm, tn, out_dtype=F32):
    M, K = a.shape
    N = w.shape[1]
    assert M % tm == 0 and N % tn == 0, (M, N, tm, tn)
    est = 2 * (tm * K * a.dtype.itemsize + K * tn * w.dtype.itemsize + tm * tn * 4) + tm * K * 2 + K * tn * 2
    return pl.pallas_call(
        _mm_body,
        grid=(M // tm, N // tn),
        in_specs=[pl.BlockSpec((tm, K), lambda i, j: (i, 0)),
                  pl.BlockSpec((K, tn), lambda i, j: (0, j))],
        out_specs=pl.BlockSpec((tm, tn), lambda i, j: (i, j)),
        out_shape=jax.ShapeDtypeStruct((M, N), out_dtype),
        compiler_params=pltpu.CompilerParams(dimension_semantics=("parallel", "parallel"),
                                             vmem_limit_bytes=_vmem_limit(est)),
        name="proj_matmul",
    )(a, w)


def _unit_lower_inverses(a_list):
    n = a_list[0].shape[0]
    ii = lax.broadcasted_iota(jnp.int32, (n, n), 0)
    jj = lax.broadcasted_iota(jnp.int32, (n, n), 1)
    eye = jnp.where(ii == jj, 1.0, 0.0)
    mm = lambda a, b: jnp.dot(a.astype(BF16), b.astype(BF16), preferred_element_type=F32)
    pows = [-a for a in a_list]
    prods = [eye + p for p in pows]
    for _ in range(n.bit_length() - 2):
        pows = [mm(p, p) for p in pows]
        prods = [pr + mm(pr, p) for pr, p in zip(prods, pows)]
    return prods


def _gdn_body(pq_ref, pk_ref, pv_ref, pz_ref, cwq_ref, cwk_ref, cwv_ref, gcol_ref, grow_ref, nw_ref, o_ref,
              xpad_scr, q_ref, k_ref, v_ref, u_scr, wq_scr, kd_scr, qk_scr, cd_scr,
              *, n_chunks, n_ctx_chunks, chunks_per_iter):
    C = CHUNK
    Lt = n_chunks * C
    n_ctx = n_ctx_chunks * C
    RB = GDN_PREP_ROWS
    PAD = V7X_SUBLANES

    def prep(src_ref, w_ref, dst_ref, unit_rows, scale):
        xpad_scr[0:PAD, :] = jnp.zeros((PAD, GDN_HEAD_DIM), F32)
        xpad_scr[PAD + Lt:2 * PAD + Lt, :] = jnp.zeros((PAD, GDN_HEAD_DIM), F32)
        xpad_scr[PAD:PAD + Lt, :] = src_ref[0]
        for blk in range(Lt // RB):
            r0 = blk * RB
            t = r0 + lax.broadcasted_iota(jnp.int32, (RB, 1), 0)
            seg_lo = jnp.where(t < n_ctx, 0, n_ctx)
            seg_hi = jnp.where(t < n_ctx, n_ctx, Lt)
            acc = jnp.zeros((RB, GDN_HEAD_DIM), F32)
            for i in range(CONV_W):
                off = i - CONV_W // 2
                xs = xpad_scr[PAD + r0 + off:PAD + r0 + off + RB, :]
                if off != 0:
                    xs = jnp.where(jnp.logical_and(t + off >= seg_lo, t + off < seg_hi), xs, 0.0)
                acc = acc + xs * w_ref[0, i:i + 1, :]
            y = acc * jax.nn.sigmoid(acc)
            if unit_rows:
                y = y * (lax.rsqrt(jnp.sum(y * y, axis=1, keepdims=True) + RMS_EPS) * scale)
            dst_ref[r0:r0 + RB, :] = y

    prep(pq_ref, cwq_ref, q_ref, True, GDN_HEAD_DIM ** -0.5)
    prep(pk_ref, cwk_ref, k_ref, True, 1.0)
    prep(pv_ref, cwv_ref, v_ref, False, 1.0)

    dn_t = (((1,), (1,)), ((), ()))
    ii = lax.broadcasted_iota(jnp.int32, (C, C), 0)
    jj = lax.broadcasted_iota(jnp.int32, (C, C), 1)
    incl = (ii >= jj, ii <= jj)
    strict = (ii > jj, ii < jj)

    def pre_iter(it, carry):
        chunks = [it * chunks_per_iter + s for s in range(chunks_per_iter)]
        rows = [pl.multiple_of(c * C, C) for c in chunks]
        qs = [q_ref[pl.ds(r0, C), :] for r0 in rows]
        ks = [k_ref[pl.ds(r0, C), :] for r0 in rows]
        vs = [v_ref[pl.ds(r0, C), :] for r0 in rows]
        kbs = [k.astype(BF16) for k in ks]
        kks = [lax.dot_general(kb, kb, dn_t, preferred_element_type=F32) for kb in kbs]
        qk_raws = [lax.dot_general(q.astype(BF16), kb, dn_t, preferred_element_type=F32) for q, kb in zip(qs, kbs)]
        combos = [(s, d) for s in range(chunks_per_iter) for d in (0, 1)]
        terms = []
        for s, d in combos:
            r0, c = rows[s], chunks[s]
            g_c = gcol_ref[0, 0, pl.ds(r0, C), d:d + 1]
            b_c = gcol_ref[0, 0, pl.ds(r0, C), 2 + d:3 + d]
            g_r = grow_ref[0, 0, d, pl.ds(c, 1), :]
            cum_col = jnp.sum(jnp.where(incl[d], g_r, 0.0), axis=1, keepdims=True)
            cum_row = jnp.sum(jnp.where(incl[1 - d], g_c, 0.0), axis=0, keepdims=True)
            cum_last = jnp.sum(g_r, axis=1, keepdims=True)
            decay = jnp.where(incl[d], jnp.exp(jnp.where(incl[d], cum_col - cum_row, 0.0)), 0.0)
            a_mat = jnp.where(strict[d], b_c * kks[s] * decay, 0.0)
            terms.append((b_c, cum_col, cum_last, decay, a_mat))
        t_invs = _unit_lower_inverses([t[4] for t in terms])
        sols = []
        for (s, d), (b_c, cum_col, cum_last, decay, _), t_inv in zip(combos, terms, t_invs):
            e_g = jnp.exp(cum_col)
            rhs = jnp.concatenate([vs[s] * b_c, ks[s] * (b_c * e_g)], axis=1)
            sols.append((e_g, jnp.dot(t_inv.astype(BF16), rhs.astype(BF16), preferred_element_type=F32)))
        for (s, d), (b_c, cum_col, cum_last, decay, _), (e_g, sol) in zip(combos, terms, sols):
            r0, c = rows[s], chunks[s]
            r1 = pl.multiple_of(c * 2 * C, 2 * C)
            u_scr[d, pl.ds(r0, C), :] = sol[:, :GDN_HEAD_DIM]
            wq_scr[d, pl.ds(r1, C), :] = sol[:, GDN_HEAD_DIM:].astype(BF16)
            wq_scr[d, pl.ds(r1 + C, C), :] = (qs[s] * e_g).astype(BF16)
            kd_scr[d, pl.ds(r0, C), :] = (ks[s] * jnp.exp(cum_last - cum_col)).astype(BF16)
            qk_scr[d, pl.ds(r0, C), :] = (qk_raws[s] * decay).astype(BF16)
            cd_scr[d, pl.ds(c, 1), :] = jnp.broadcast_to(jnp.exp(cum_last), (1, GDN_HEAD_DIM))
        return carry

    lax.fori_loop(0, n_chunks // chunks_per_iter, pre_iter, 0)

    o_ref[...] = jnp.zeros_like(o_ref)

    def step(t, states):
        c_bwd = jnp.where(t < n_ctx_chunks, n_ctx_chunks - 1 - t, n_chunks - 1 + n_ctx_chunks - t)
        chunks = (t, c_bwd)
        rows = [pl.multiple_of(c * C, C) for c in chunks]
        ws_qs = [jnp.dot(wq_scr[d, pl.ds(pl.multiple_of(chunks[d] * 2 * C, 2 * C), 2 * C), :],
                         states[d].astype(BF16), preferred_element_type=F32) for d in (0, 1)]
        v_new_b = [(u_scr[d, pl.ds(rows[d], C), :] - ws_qs[d][:C]).astype(BF16) for d in (0, 1)]
        intra = [jnp.dot(qk_scr[d, pl.ds(rows[d], C), :], v_new_b[d], preferred_element_type=F32) for d in (0, 1)]
        upd = [lax.dot_general(kd_scr[d, pl.ds(rows[d], C), :], v_new_b[d], (((0,), (0,)), ((), ())),
                               preferred_element_type=F32) for d in (0, 1)]
        for d in (0, 1):
            o_ref[0, pl.ds(rows[d], C), :] += ws_qs[d][C:] + intra[d]
        return tuple(states[d] * cd_scr[d, pl.ds(chunks[d], 1), :] + upd[d] for d in (0, 1))

    s0 = jnp.zeros((GDN_HEAD_DIM, GDN_HEAD_DIM), F32)
    lax.fori_loop(0, n_chunks, step, (s0, s0))

    for blk in range(Lt // RB):
        r0 = blk * RB
        o = o_ref[0, r0:r0 + RB, :]
        z = pz_ref[0, r0:r0 + RB, :]
        y = o * lax.rsqrt(jnp.mean(o * o, axis=1, keepdims=True) + RMS_EPS) * nw_ref[...]
        o_ref[0, r0:r0 + RB, :] = y * (z * jax.nn.sigmoid(z))


def gated_delta_heads(p, conv_w, gcol, grow, norm_w, *, n_ctx_chunks):
    B, Lt = p.shape[:2]
    H, dk = GDN_HEADS, GDN_HEAD_DIM
    n_chunks = Lt // CHUNK
    chunks_per_iter = GDN_CHUNKS_PER_ITER
    assert n_chunks % chunks_per_iter == 0 and Lt % GDN_PREP_ROWS == 0
    cd_rows = -(-n_chunks // 8) * 8
    col_spec = lambda first: pl.BlockSpec((1, Lt, dk), lambda b, h: (b, 0, first + h))
    w_spec = lambda first: pl.BlockSpec((1, CONV_W, dk), lambda b, h: (first + h, 0, 0))
    scratch = [pltpu.VMEM((Lt + 2 * V7X_SUBLANES, dk), F32),
               pltpu.VMEM((Lt, dk), F32),
               pltpu.VMEM((Lt, dk), F32),
               pltpu.VMEM((Lt, dk), F32),
               pltpu.VMEM((2, Lt, dk), F32),
               pltpu.VMEM((2, 2 * Lt, dk), BF16),
               pltpu.VMEM((2, Lt, dk), BF16),
               pltpu.VMEM((2, Lt, CHUNK), BF16),
               pltpu.VMEM((2, cd_rows, dk), F32)]
    est = (2 * 5 * Lt * dk * 4
           + 2 * Lt * V7X_LANES * 4
           + 4 * Lt * dk * 4
           + 2 * Lt * dk * 4 + 2 * 2 * Lt * dk * 2 + 2 * Lt * dk * 2 + 2 * Lt * V7X_LANES * 2)
    body = functools.partial(_gdn_body, n_chunks=n_chunks, n_ctx_chunks=n_ctx_chunks,
                             chunks_per_iter=chunks_per_iter)
    return pl.pallas_call(
        body,
        grid=(B, H),
        in_specs=[col_spec(0), col_spec(H), col_spec(2 * H), col_spec(3 * H),
                  w_spec(0), w_spec(H), w_spec(2 * H),
                  pl.BlockSpec((1, 1, Lt, 4), lambda b, h: (b, h, 0, 0)),
                  pl.BlockSpec((1, 1, 2, n_chunks, CHUNK), lambda b, h: (b, h, 0, 0, 0)),
                  pl.BlockSpec((1, dk), lambda b, h: (0, 0))],
        out_specs=pl.BlockSpec((1, Lt, dk), lambda b, h: (b, 0, h)),
        out_shape=jax.ShapeDtypeStruct((B, Lt, H * dk), F32),
        scratch_shapes=scratch,
        compiler_params=pltpu.CompilerParams(dimension_semantics=("parallel", "parallel"),
                                             vmem_limit_bytes=int(min(VMEM_CAP, est + est // 8))),
        name="gated_delta_scan",
    )(p, p, p, p, conv_w, conv_w, conv_w, gcol, grow, norm_w)


def _attn_body(q_ref, kt_ref, v_ref, o_ref, s_scr, *, n_ctx, first_tile):
    G, tq, hd = q_ref.shape[2:]
    rows = G * tq
    tk = s_scr.shape[2]
    tile = pl.program_id(2) + first_tile
    q = q_ref[0, 0].reshape(rows, hd)

    def run(n_blocks):
        m = None
        for j in range(n_blocks):
            s = jnp.dot(q, kt_ref[0, 0, :, j * tk:(j + 1) * tk], preferred_element_type=F32)
            s_scr[j] = s
            bm = jnp.max(s, axis=1, keepdims=True)
            m = bm if m is None else jnp.maximum(m, bm)
        denom = jnp.zeros((rows, 1), F32)
        acc = jnp.zeros((rows, hd), F32)
        for j in range(n_blocks):
            p = jnp.exp(s_scr[j] - m)
            denom = denom + jnp.sum(p, axis=1, keepdims=True)
            acc = acc + jnp.dot(p.astype(BF16), v_ref[0, 0, j * tk:(j + 1) * tk, :], preferred_element_type=F32)
        o = acc / denom
        for g in range(G):
            o_ref[0, :, g * hd:(g + 1) * hd] = o[g * tq:(g + 1) * tq]

    if first_tile * tq < n_ctx:
        pl.when(tile * tq < n_ctx)(lambda: run(n_ctx // tk))
        pl.when(tile * tq >= n_ctx)(lambda: run(v_ref.shape[2] // tk))
    else:
        run(v_ref.shape[2] // tk)


def attention(q, kt, v, *, n_ctx, skip_ctx_queries):
    B, Hkv, G, Lt, hd = q.shape
    tq = ATT_Q_TILE
    assert Lt % tq == 0 and n_ctx % tq == 0
    first_tile = n_ctx // tq if skip_ctx_queries else 0
    n_tiles = Lt // tq - first_tile
    tk = ATT_KV_TILE
    assert Lt % tk == 0 and n_ctx % tk == 0
    rows = G * tq
    scratch = [pltpu.VMEM((Lt // tk, rows, tk), F32)]
    est = 2 * (rows * hd * 2 + 2 * Lt * hd * 2 + tq * G * hd * 4) + rows * Lt * 4 + 8 * rows * tk * 4
    body = functools.partial(_attn_body, n_ctx=n_ctx, first_tile=first_tile)
    return pl.pallas_call(
        body,
        grid=(B, Hkv, n_tiles),
        in_specs=[pl.BlockSpec((1, 1, G, tq, hd), lambda b, h, i: (b, h, 0, i + first_tile, 0)),
                  pl.BlockSpec((1, 1, hd, Lt), lambda b, h, i: (b, h, 0, 0)),
                  pl.BlockSpec((1, 1, Lt, hd), lambda b, h, i: (b, h, 0, 0))],
        out_specs=pl.BlockSpec((1, tq, G * hd), lambda b, h, i: (b, i, h)),
        out_shape=jax.ShapeDtypeStruct((B, n_tiles * tq, Hkv * G * hd), F32),
        scratch_shapes=scratch,
        compiler_params=pltpu.CompilerParams(dimension_semantics=("parallel", "parallel", "parallel"),
                                             vmem_limit_bytes=_vmem_limit(est // 2)),
        name="gqa_attention",
    )(q, kt, v)


def _rows_2d(x_ref):
    tm = x_ref.shape[0] // V7X_SUBLANES
    return jnp.concatenate([x_ref[pl.ds(c, tm, stride=V7X_SUBLANES), :] for c in range(V7X_SUBLANES)], axis=1)


def _router_body(h_ref, w_ref, bias_ref, eidx_ref, wts_ref, rank_ref, counts_ref, carry_scr):
    tm = h_ref.shape[0] // V7X_SUBLANES
    E = N_EXPERTS
    per_group = E // N_GROUPS
    neg = -jnp.inf

    @pl.when(pl.program_id(0) == 0)
    def _():
        carry_scr[...] = jnp.zeros_like(carry_scr)

    logits = jnp.dot(_rows_2d(h_ref).astype(BF16), w_ref[...], preferred_element_type=F32)
    s = jax.nn.sigmoid(logits)
    sel = s + bias_ref[...]
    lane = lax.broadcasted_iota(jnp.int32, (tm, E), 1)
    lane_f = lane.astype(F32)
    grp = lane // per_group
    scores = []
    for g in range(N_GROUPS):
        v = jnp.where(grp == g, sel, neg)
        m1 = jnp.max(v, axis=1, keepdims=True)
        i1 = jnp.min(jnp.where(v == m1, lane_f, float(E)), axis=1, keepdims=True)
        m2 = jnp.max(jnp.where(lane_f == i1, neg, v), axis=1, keepdims=True)
        scores.append(m1 + m2)
    allowed = jnp.zeros((tm, E), jnp.bool_)
    for g in range(N_GROUPS):
        beaten = jnp.zeros((tm, 1), F32)
        for o in range(N_GROUPS):
            if o != g:
                wins = (scores[o] > scores[g]) if o > g else (scores[o] >= scores[g])
                beaten = beaten + wins.astype(F32)
        allowed = jnp.logical_or(allowed, jnp.logical_and(grp == g, beaten < TOPK_GROUPS))
    masked = jnp.where(allowed, sel, neg)
    idx_cols, s_cols = [], []
    chosen = jnp.zeros((tm, E), jnp.bool_)
    for _ in range(TOP_K):
        m = jnp.max(masked, axis=1, keepdims=True)
        idx = jnp.min(jnp.where(masked == m, lane_f, float(E)), axis=1, keepdims=True)
        hit = lane_f == idx
        s_cols.append(jnp.sum(jnp.where(hit, s, 0.0), axis=1, keepdims=True))
        idx_cols.append(idx)
        chosen = jnp.logical_or(chosen, hit)
        masked = jnp.where(hit, neg, masked)
    ri = lax.broadcasted_iota(jnp.int32, (tm, tm), 0)
    ci = lax.broadcasted_iota(jnp.int32, (tm, tm), 1)
    running = jnp.dot((ci < ri).astype(BF16), chosen.astype(BF16), preferred_element_type=F32) + carry_scr[...]
    total = s_cols[0]
    for col in s_cols[1:]:
        total = total + col
    k_lane = lax.broadcasted_iota(jnp.int32, (tm, TOP_K), 1)
    eidx = jnp.zeros((tm, TOP_K), F32)
    wts = jnp.zeros((tm, TOP_K), F32)
    rank = jnp.zeros((tm, TOP_K), F32)
    for j in range(TOP_K):
        r_j = jnp.sum(jnp.where(lane_f == idx_cols[j], running, 0.0), axis=1, keepdims=True)
        eidx = jnp.where(k_lane == j, idx_cols[j], eidx)
        wts = jnp.where(k_lane == j, s_cols[j] / total * ROUTED_SCALE, wts)
        rank = jnp.where(k_lane == j, r_j, rank)
    eidx_ref[...] = eidx.astype(jnp.int32)
    wts_ref[...] = wts
    rank_ref[...] = rank.astype(jnp.int32)
    carry_scr[...] = carry_scr[...] + jnp.sum(chosen.astype(F32), axis=0, keepdims=True)
    counts_ref[...] = carry_scr[...].astype(jnp.int32)


def moe_router(h, router_w_bf, router_bias):
    T = h.shape[0]
    D, E = router_w_bf.shape
    tm = ROUTER_ROWS
    assert T % tm == 0
    tok = pl.BlockSpec((tm, TOP_K), lambda i: (i, 0))
    est = 2 * (tm * D * 4 + D * E * 2) + tm * tm * 4 + 24 * tm * E * 4
    h = h.reshape(T * V7X_SUBLANES, V7X_LANES)
    return pl.pallas_call(
        _router_body,
        grid=(T // tm,),
        in_specs=[pl.BlockSpec((tm * V7X_SUBLANES, V7X_LANES), lambda i: (i, 0)),
                  pl.BlockSpec((D, E), lambda i: (0, 0)),
                  pl.BlockSpec((1, E), lambda i: (0, 0))],
        out_specs=[tok, tok, tok, pl.BlockSpec((1, E), lambda i: (0, 0))],
        out_shape=[jax.ShapeDtypeStruct((T, TOP_K), jnp.int32), jax.ShapeDtypeStruct((T, TOP_K), F32),
                   jax.ShapeDtypeStruct((T, TOP_K), jnp.int32), jax.ShapeDtypeStruct((1, E), jnp.int32)],
        scratch_shapes=[pltpu.VMEM((1, E), F32)],
        compiler_params=pltpu.CompilerParams(dimension_semantics=("arbitrary",),
                                             vmem_limit_bytes=_vmem_limit(est // 2)),
        name="moe_router",
    )(h, router_w_bf, router_bias.reshape(1, E))


def _dispatch_body(pstart_ref, pad_lo_ref, pad_hi_ref, eidx_ref, rank_ref, h_ref, xs_hbm, zero_scr, sems):
    i = pl.program_id(0)
    tt = h_ref.shape[0]

    def row_copy(src, dst_row, sem):
        return pltpu.make_async_copy(src, xs_hbm.at[dst_row], sem)

    @pl.when(i == 0)
    def _():
        zero_scr[...] = jnp.zeros_like(zero_scr)
        for wait in (False, True):
            def per_expert(e, carry, wait=wait):
                def per_row(r, c):
                    cp = row_copy(zero_scr, r, sems.at[1])
                    cp.wait() if wait else cp.start()
                    return c
                return lax.fori_loop(pad_lo_ref[e], pad_hi_ref[e], per_row, carry)
            lax.fori_loop(0, N_EXPERTS, per_expert, 0)

    def issue(p, c):
        toks = [p * DMA_TOKENS_PER_ITER + u for u in range(DMA_TOKENS_PER_ITER)]
        dst = [[pstart_ref[eidx_ref[r * TOP_K + j]] + rank_ref[r * TOP_K + j] for j in range(TOP_K)]
               for r in toks]
        for r, dst_r in zip(toks, dst):
            for d in dst_r:
                row_copy(h_ref.at[r], d, sems.at[0]).start()
        return c
    lax.fori_loop(0, tt // DMA_TOKENS_PER_ITER, issue, 0)

    def drain(r, c):
        for j in range(TOP_K):
            row_copy(h_ref.at[0], 0, sems.at[0]).wait()
        return c
    lax.fori_loop(0, tt, drain, 0)


def moe_dispatch(h, eidx_flat, rank_flat, pad_starts, pad_lo, pad_hi, n_rows):
    T = h.shape[0]
    tile = h.shape[1:]
    tt = DISPATCH_TOKENS
    assert T % tt == 0
    idx_spec = pl.BlockSpec((tt * TOP_K,), lambda i, *_: (i,), memory_space=pltpu.SMEM)
    grid_spec = pltpu.PrefetchScalarGridSpec(
        num_scalar_prefetch=3, grid=(T // tt,),
        in_specs=[idx_spec, idx_spec, pl.BlockSpec((tt,) + tile, lambda i, *_: (i, 0, 0))],
        out_specs=pl.BlockSpec(memory_space=pl.ANY),
        scratch_shapes=[pltpu.VMEM(tile, h.dtype), pltpu.SemaphoreType.DMA((2,))])
    return pl.pallas_call(
        _dispatch_body,
        grid_spec=grid_spec,
        out_shape=jax.ShapeDtypeStruct((n_rows,) + tile, h.dtype),
        compiler_params=pltpu.CompilerParams(dimension_semantics=("arbitrary",)),
        name="moe_dispatch",
    )(pad_starts, pad_lo, pad_hi, eidx_flat, rank_flat, h)


def _combine_body(pstart_ref, eidx_ref, rank_ref, wts_ref, shared_ref, rows_hbm, o_ref, buf, sems, *, tt, n_tiles):
    i = pl.program_id(0)

    def row_copy(src_row, slot, j, r):
        return pltpu.make_async_copy(rows_hbm.at[src_row], buf.at[slot, j, r], sems.at[slot])

    @pl.when(i < n_tiles)
    def _():
        slot = i % 2

        def issue(p, c):
            toks = [p * DMA_TOKENS_PER_ITER + u for u in range(DMA_TOKENS_PER_ITER)]
            src = [[pstart_ref[eidx_ref[r * TOP_K + j]] + rank_ref[r * TOP_K + j] for j in range(TOP_K)]
                   for r in toks]
            for r, src_r in zip(toks, src):
                for j, s in enumerate(src_r):
                    row_copy(s, slot, j, r).start()
            return c
        lax.fori_loop(0, tt // DMA_TOKENS_PER_ITER, issue, 0)

    @pl.when(i >= 1)
    def _():
        slot = (i + 1) % 2

        def drain(r, c):
            for j in range(TOP_K):
                row_copy(0, slot, j, 0).wait()
            return c
        lax.fori_loop(0, tt, drain, 0)

        def per_token(r, c):
            acc = shared_ref[r]
            for j in range(TOP_K):
                acc = acc + buf[slot, j, r] * wts_ref[r * TOP_K + j]
            o_ref[r] = acc
            return c
        lax.fori_loop(0, tt, per_token, 0, unroll=4)


def moe_combine(rows, eidx_flat, rank_flat, pad_starts, wts_flat, shared):
    T = shared.shape[0]
    tile = shared.shape[1:]
    tt = COMBINE_TOKENS
    assert T % tt == 0
    n_tiles = T // tt
    prev = lambda i, *_: (jnp.maximum(i - 1, 0), 0, 0)
    idx_spec = pl.BlockSpec((tt * TOP_K,), lambda i, *_: (jnp.minimum(i, n_tiles - 1),), memory_space=pltpu.SMEM)
    wts_spec = pl.BlockSpec((tt * TOP_K,), lambda i, *_: (jnp.maximum(i - 1, 0),), memory_space=pltpu.SMEM)
    grid_spec = pltpu.PrefetchScalarGridSpec(
        num_scalar_prefetch=1, grid=(n_tiles + 1,),
        in_specs=[idx_spec, idx_spec, wts_spec, pl.BlockSpec((tt,) + tile, prev),
                  pl.BlockSpec(memory_space=pl.ANY)],
        out_specs=pl.BlockSpec((tt,) + tile, prev),
        scratch_shapes=[pltpu.VMEM((2, TOP_K, tt) + tile, F32), pltpu.SemaphoreType.DMA((2,))])
    row_bytes = tile[0] * tile[1] * 4
    est = 2 * TOP_K * tt * row_bytes + 4 * tt * row_bytes
    return pl.pallas_call(
        functools.partial(_combine_body, tt=tt, n_tiles=n_tiles),
        grid_spec=grid_spec,
        out_shape=jax.ShapeDtypeStruct((T,) + tile, F32),
        compiler_params=pltpu.CompilerParams(dimension_semantics=("arbitrary",),
                                             vmem_limit_bytes=_vmem_limit(est)),
        name="moe_combine",
    )(pad_starts, eidx_flat, rank_flat, wts_flat, shared, rows)


def _expert_body(blk_e_ref, n_used_ref, x_ref, wg_ref, wu_ref, wd_ref, o_ref, wg_b, wu_b, wd_b):
    i = pl.program_id(0)

    @pl.when(jnp.logical_or(i == 0, blk_e_ref[i] != blk_e_ref[jnp.maximum(i - 1, 0)]))
    def _():
        wg_b[...] = wg_ref[0, 0].astype(BF16)
        wu_b[...] = wu_ref[0, 0].astype(BF16)
        wd_b[...] = wd_ref[0, 0].astype(BF16)

    @pl.when(i < n_used_ref[0])
    def _():
        x = _rows_2d(x_ref).astype(BF16)
        a = jnp.dot(x, wg_b[...], preferred_element_type=F32)
        b = jnp.dot(x, wu_b[...], preferred_element_type=F32)
        h = (a * jax.nn.sigmoid(a)) * b
        o = jnp.dot(h.astype(BF16), wd_b[...], preferred_element_type=F32)
        tm = o.shape[0]
        for c in range(V7X_SUBLANES):
            o_ref[pl.ds(c, tm, stride=V7X_SUBLANES), :] = o[:, c * V7X_LANES:(c + 1) * V7X_LANES]


def expert_mlp(xs, blk_e, n_used, layer, w_gate, w_up, w_down):
    P = xs.shape[0]
    tile = xs.shape[1:]
    D = tile[0] * tile[1]
    F = w_gate.shape[-1]
    tm = EXPERT_ROWS
    n_blocks = P // tm
    xs = xs.reshape(P * tile[0], tile[1])

    def row_map(i, blk_e_ref, n_used_ref):
        return (jnp.minimum(i, n_used_ref[0] - 1), 0)

    def w_map(i, blk_e_ref, n_used_ref):
        return (layer, blk_e_ref[i], 0, 0)

    est = 2 * (2 * tm * D * 4 + 3 * D * F * 4) + 3 * D * F * 2 + tm * D * 2 + 3 * tm * F * 4
    grid_spec = pltpu.PrefetchScalarGridSpec(
        num_scalar_prefetch=2,
        grid=(n_blocks,),
        in_specs=[pl.BlockSpec((tm * tile[0], tile[1]), row_map),
                  pl.BlockSpec((1, 1, D, F), w_map),
                  pl.BlockSpec((1, 1, D, F), w_map),
                  pl.BlockSpec((1, 1, F, D), w_map)],
        out_specs=pl.BlockSpec((tm * tile[0], tile[1]), row_map),
        scratch_shapes=[pltpu.VMEM((D, F), BF16), pltpu.VMEM((D, F), BF16), pltpu.VMEM((F, D), BF16)],
    )
    out = pl.pallas_call(
        _expert_body,
        grid_spec=grid_spec,
        out_shape=jax.ShapeDtypeStruct(xs.shape, F32),
        compiler_params=pltpu.CompilerParams(dimension_semantics=("arbitrary",),
                                             vmem_limit_bytes=_vmem_limit(est)),
        name="expert_mlp",
    )(blk_e, n_used, xs, w_gate, w_up, w_down)
    return out.reshape((P,) + tile)


def _standardize(x, eps):
    mu = jnp.mean(x, -1, keepdims=True)
    var = jnp.mean(jnp.square(x - mu), -1, keepdims=True)
    return (x - mu) * lax.rsqrt(var + eps)


def _layer_norm(x, w, b):
    return _standardize(x, LN_EPS) * w + b


def _rms_norm(x, w):
    return x * lax.rsqrt(jnp.mean(x * x, -1, keepdims=True) + RMS_EPS) * w


def _rope_tables(n_ctx, n_lat):
    rows = n_lat // GRID_W
    row = jnp.repeat(jnp.arange(rows, dtype=F32), GRID_W)
    col = jnp.tile(jnp.arange(GRID_W, dtype=F32), rows)
    inv_freq = ROPE_THETA ** (-jnp.arange(0, ROPE_AXIS_DIM, 2, dtype=F32) / ROPE_AXIS_DIM)
    ang = jnp.stack([row[:, None] * inv_freq, col[:, None] * inv_freq], axis=1)
    cos = jnp.concatenate([jnp.ones((n_ctx,) + ang.shape[1:], F32), jnp.cos(ang)], axis=0)
    sin = jnp.concatenate([jnp.zeros((n_ctx,) + ang.shape[1:], F32), jnp.sin(ang)], axis=0)
    return cos, sin


def _apply_rope(x, cos, sin):
    B, Lt, H, hd = x.shape
    F = ROPE_AXIS_DIM // 2
    xf = x.reshape(B, Lt, H, 2, 2, F)
    x1, x2 = xf[..., 0, :], xf[..., 1, :]
    c, s = cos[None, :, None], sin[None, :, None]
    return jnp.stack([x1 * c - x2 * s, x2 * c + x1 * s], axis=-2).reshape(B, Lt, H, hd)


def _mixer(h_bf, w_main, w_ba, conv_w, a_log, dt_bias, gdn_norm_w, q_norm_w, k_norm_w, w_out_bf, cos, sin,
           *, B, Lt, n_ctx, skip_ctx_queries):
    T = B * Lt
    p = matmul(h_bf, w_main, tm=MM_ROWS, tn=1024)
    p_ba = matmul(h_bf, w_ba, tm=MM_ROWS, tn=V7X_LANES)
    p = p.reshape(B, Lt, -1)
    ba = p_ba[:, :4 * GDN_HEADS].reshape(B, Lt, 2, 2, GDN_HEADS)
    beta = jax.nn.sigmoid(ba[:, :, 0])
    g = -jnp.exp(a_log) * jax.nn.softplus(ba[:, :, 1] + dt_bias)
    gcol = jnp.concatenate([g, beta], axis=2).transpose(0, 3, 1, 2)
    grow = g.transpose(0, 3, 2, 1).reshape(B, GDN_HEADS, 2, Lt // CHUNK, CHUNK)
    conv_heads = conv_w.reshape(CONV_W, 3 * GDN_HEADS, GDN_HEAD_DIM).transpose(1, 0, 2)
    gdn = gated_delta_heads(p, conv_heads, gcol, grow, gdn_norm_w.reshape(1, GDN_HEAD_DIM),
                            n_ctx_chunks=n_ctx // CHUNK)
    a = p[..., OFF_BA:]
    qa = _rms_norm(a[..., :ATT_WIDTH].reshape(B, Lt, ATT_Q_HEADS, ATT_HEAD_DIM), q_norm_w)
    ka = _rms_norm(a[..., ATT_WIDTH:ATT_WIDTH + ATT_KV_WIDTH].reshape(B, Lt, ATT_KV_HEADS, ATT_HEAD_DIM), k_norm_w)
    va = a[..., ATT_WIDTH + ATT_KV_WIDTH:].reshape(B, Lt, ATT_KV_HEADS, ATT_HEAD_DIM)
    qa = _apply_rope(qa, cos, sin) * ATT_HEAD_DIM ** -0.5
    ka = _apply_rope(ka, cos, sin)
    qa = qa.astype(BF16).reshape(B, Lt, ATT_KV_HEADS, ATT_GROUP, ATT_HEAD_DIM).transpose(0, 2, 3, 1, 4)
    ka = ka.astype(BF16).transpose(0, 2, 3, 1)
    va = va.astype(BF16).transpose(0, 2, 1, 3)
    att = attention(qa, ka, va, n_ctx=n_ctx, skip_ctx_queries=skip_ctx_queries)
    if skip_ctx_queries:
        gdn = gdn[:, n_ctx:]
    mix = jnp.concatenate([gdn, att], axis=-1).astype(BF16)
    return matmul(mix.reshape(-1, mix.shape[-1]), w_out_bf, tm=MM_ROWS, tn=1024)


def _moe(h, layer, router_w_bf, router_bias, w_gate, w_up, w_down, sh_gate, sh_up, sh_down):
    T = h.shape[0]
    E = N_EXPERTS
    tm = EXPERT_ROWS
    eidx, wts, rank, counts = moe_router(h, router_w_bf, router_bias)
    counts = counts[0]
    padded = (counts + tm - 1) // tm * tm
    pad_ends = jnp.cumsum(padded)
    pad_starts = pad_ends - padded
    n_blocks = -(-T * TOP_K // tm) + E
    n_used = (pad_ends[-1] // tm).astype(jnp.int32)
    blk_first_row = jnp.minimum(jnp.arange(n_blocks, dtype=jnp.int32), n_used - 1) * tm
    blk_e = jnp.sum((pad_ends[None, :] <= blk_first_row[:, None]).astype(jnp.int32), axis=1)
    blk_e = jnp.minimum(blk_e, E - 1)
    eidx_flat, rank_flat = eidx.reshape(-1), rank.reshape(-1)
    xs = moe_dispatch(h, eidx_flat, rank_flat, pad_starts, pad_starts + counts, pad_ends, n_blocks * tm)
    out_rows = expert_mlp(xs, blk_e, n_used.reshape(1), layer, w_gate, w_up, w_down)
    n_sh = T // tm
    shared = expert_mlp(h, jnp.zeros((n_sh,), jnp.int32), jnp.full((1,), n_sh, jnp.int32), layer,
                        sh_gate[:, None], sh_up[:, None], sh_down[:, None])
    return moe_combine(out_rows, eidx_flat, rank_flat, pad_starts, wts.reshape(-1), shared)


def kernel(x, c, ctx, c_ctx, ada_w, ada_b, w_in, conv_w, gdn_a_log, gdn_dt_bias, gdn_norm_w, q_norm_w, k_norm_w,
           w_out, ln1_w, ln1_b, router_w, router_bias, exp_w_gate, exp_w_up, exp_w_down, sh_w_gate, sh_w_up,
           sh_w_down, ln2_w, ln2_b):
    B, L, D = x.shape
    Lc = ctx.shape[1]
    Lt = Lc + L
    depth = ada_w.shape[0]
    alpha = (2.0 * depth) ** 0.25
    cos, sin = _rope_tables(Lc, L)
    xa = _standardize(jnp.concatenate([ctx, x], axis=1), LN_EPS)
    is_ctx = (jnp.arange(Lt) < Lc)[None, :, None]
    cond = jnp.concatenate([c, c_ctx[None], jnp.zeros((16 - B - 1, D), F32)], axis=0)
    cond = jax.nn.silu(cond)
    for l in range(depth):
        last = l == depth - 1
        mod_all = matmul(cond, ada_w[l], tm=16, tn=1024) + ada_b[l]
        mod, mod_c = mod_all[:B], mod_all[B]
        pick = lambda i: jnp.where(is_ctx, mod_c[None, None, i * D:(i + 1) * D], mod[:, None, i * D:(i + 1) * D])
        sh1, sc1, g1, sh2, sc2, g2 = (pick(i) for i in range(6))
        w_l = w_in[l]
        w_main = jnp.concatenate([w_l[:, :OFF_BA], w_l[:, OFF_ATT:]], axis=1).astype(BF16)
        w_ba = jnp.pad(w_l[:, OFF_BA:OFF_ATT], ((0, 0), (0, V7X_LANES - 4 * GDN_HEADS))).astype(BF16)
        h = (xa * (1.0 + sc1) + sh1).astype(BF16).reshape(B * Lt, D)
        y = _mixer(h, w_main, w_ba, conv_w[l], gdn_a_log[l], gdn_dt_bias[l], gdn_norm_w[l], q_norm_w[l],
                   k_norm_w[l], w_out[l].astype(BF16), cos, sin, B=B, Lt=Lt, n_ctx=Lc, skip_ctx_queries=last)
        if last:
            xa, g1, sc2, sh2, g2 = (t[:, Lc:] for t in (xa, g1, sc2, sh2, g2))
        rows = xa.shape[1]
        xa = _layer_norm(alpha * xa + g1 * y.reshape(B, rows, D), ln1_w[l], ln1_b[l])
        h2 = (xa * (1.0 + sc2) + sh2).reshape(B * rows, V7X_SUBLANES, D // V7X_SUBLANES)
        ff = _moe(h2, l, router_w[l].astype(BF16), router_bias[l], exp_w_gate, exp_w_up, exp_w_down,
                  sh_w_gate, sh_w_up, sh_w_down)
        xa = _layer_norm(alpha * xa + g2 * ff.reshape(B, rows, D), ln2_w[l], ln2_b[l])
    return xa
```

```python
import functools

import jax
import jax.numpy as jnp
from jax import lax
from jax.experimental import pallas as pl
from jax.experimental.pallas import tpu as pltpu

F32 = jnp.float32
BF16 = jnp.bfloat16

GRID_W = 64
GDN_HEAD_DIM = 128
GDN_HEADS = 4
GDN_WIDTH = GDN_HEADS * GDN_HEAD_DIM
CONV_W = 5
CHUNK = 64
ATT_HEAD_DIM = 128
ATT_Q_HEADS = 4
ATT_KV_HEADS = 2
ATT_GROUP = ATT_Q_HEADS // ATT_KV_HEADS
ATT_WIDTH = ATT_Q_HEADS * ATT_HEAD_DIM
ATT_KV_WIDTH = ATT_KV_HEADS * ATT_HEAD_DIM
ROPE_AXIS_DIM = ATT_HEAD_DIM // 2
ROPE_THETA = 10000.0
OFF_Z = 3 * GDN_WIDTH
OFF_BA = 4 * GDN_WIDTH
OFF_ATT = OFF_BA + 4 * GDN_HEADS
N_EXPERTS = 256
TOP_K = 8
N_GROUPS = 8
TOPK_GROUPS = 4
EXPERT_DIM = 256
ROUTED_SCALE = 2.5
LN_EPS = 1e-5
RMS_EPS = 1e-6

V7X_LANES = 128
V7X_SUBLANES = 8
V7X_VMEM_BYTES = 64 * 1024 * 1024
VMEM_CAP = V7X_VMEM_BYTES - 8 * 1024 * 1024

EXPERT_ROWS = 256
GDN_CHUNKS_PER_ITER = 4
GDN_PREP_ROWS = 256
ROUTER_ROWS = 512
DISPATCH_TOKENS = 256
COMBINE_TOKENS = 128
DMA_TOKENS_PER_ITER = 2
RESNORM_ROWS = 256
ATT_Q_TILE = 256
ATT_KV_TILE = 256
MM_ROWS = 1024


def _vmem_limit(estimate_bytes):
    return int(min(VMEM_CAP, max(16 * 1024 * 1024, 2 * estimate_bytes)))


def _mm_body(a_ref, w_ref, o_ref):
    o_ref[...] = jnp.dot(a_ref[...].astype(BF16), w_ref[...].astype(BF16),
                         preferred_element_type=F32).astype(o_ref.dtype)


def matmul(a, w, *, tm, tn, out_dtype=F32):
    M, K = a.shape
    N = w.shape[1]
    assert M % tm == 0 and N % tn == 0, (M, N, tm, tn)
    est = 2 * (tm * K * a.dtype.itemsize + K * tn * w.dtype.itemsize + tm * tn * 4) + tm * K * 2 + K * tn * 2
    return pl.pallas_call(
        _mm_body,
        grid=(M // tm, N // tn),
        in_specs=[pl.BlockSpec((tm, K), lambda i, j: (i, 0)),
                  pl.BlockSpec((K, tn), lambda i, j: (0, j))],
        out_specs=pl.BlockSpec((tm, tn), lambda i, j: (i, j)),
        out_shape=jax.ShapeDtypeStruct((M, N), out_dtype),
        compiler_params=pltpu.CompilerParams(dimension_semantics=("parallel", "parallel"),
                                             vmem_limit_bytes=_vmem_limit(est)),
        name="proj_matmul",
    )(a, w)


def _unit_lower_inverses(a_list):
    n = a_list[0].shape[0]
    ii = lax.broadcasted_iota(jnp.int32, (n, n), 0)
    jj = lax.broadcasted_iota(jnp.int32, (n, n), 1)
    eye = jnp.where(ii == jj, 1.0, 0.0)
    mm = lambda a, b: jnp.dot(a.astype(BF16), b.astype(BF16), preferred_element_type=F32)
    pows = [-a for a in a_list]
    prods = [eye + p for p in pows]
    for _ in range(n.bit_length() - 2):
        pows = [mm(p, p) for p in pows]
        prods = [pr + mm(pr, p) for pr, p in zip(prods, pows)]
    return prods


def _gdn_body(pq_ref, pk_ref, pv_ref, pz_ref, cwq_ref, cwk_ref, cwv_ref, gcol_ref, grow_ref, nw_ref, o_ref,
              xpad_scr, q_ref, k_ref, v_ref, u_scr, wq_scr, kd_scr, qk_scr, cd_scr,
              *, n_chunks, n_ctx_chunks, chunks_per_iter):
    C = CHUNK
    Lt = n_chunks * C
    n_ctx = n_ctx_chunks * C
    RB = GDN_PREP_ROWS
    PAD = V7X_SUBLANES

    def prep(src_ref, w_ref, dst_ref, unit_rows, scale):
        xpad_scr[0:PAD, :] = jnp.zeros((PAD, GDN_HEAD_DIM), F32)
        xpad_scr[PAD + Lt:2 * PAD + Lt, :] = jnp.zeros((PAD, GDN_HEAD_DIM), F32)
        xpad_scr[PAD:PAD + Lt, :] = src_ref[0]
        for blk in range(Lt // RB):
            r0 = blk * RB
            t = r0 + lax.broadcasted_iota(jnp.int32, (RB, 1), 0)
            seg_lo = jnp.where(t < n_ctx, 0, n_ctx)
            seg_hi = jnp.where(t < n_ctx, n_ctx, Lt)
            acc = jnp.zeros((RB, GDN_HEAD_DIM), F32)
            for i in range(CONV_W):
                off = i - CONV_W // 2
                xs = xpad_scr[PAD + r0 + off:PAD + r0 + off + RB, :]
                if off != 0:
                    xs = jnp.where(jnp.logical_and(t + off >= seg_lo, t + off < seg_hi), xs, 0.0)
                acc = acc + xs * w_ref[0, i:i + 1, :]
            y = acc * jax.nn.sigmoid(acc)
            if unit_rows:
                y = y * (lax.rsqrt(jnp.sum(y * y, axis=1, keepdims=True) + RMS_EPS) * scale)
            dst_ref[r0:r0 + RB, :] = y

    prep(pq_ref, cwq_ref, q_ref, True, GDN_HEAD_DIM ** -0.5)
    prep(pk_ref, cwk_ref, k_ref, True, 1.0)
    prep(pv_ref, cwv_ref, v_ref, False, 1.0)

    dn_t = (((1,), (1,)), ((), ()))
    ii = lax.broadcasted_iota(jnp.int32, (C, C), 0)
    jj = lax.broadcasted_iota(jnp.int32, (C, C), 1)
    incl = (ii >= jj, ii <= jj)
    strict = (ii > jj, ii < jj)

    def pre_iter(it, carry):
        chunks = [it * chunks_per_iter + s for s in range(chunks_per_iter)]
        rows = [pl.multiple_of(c * C, C) for c in chunks]
        qs = [q_ref[pl.ds(r0, C), :] for r0 in rows]
        ks = [k_ref[pl.ds(r0, C), :] for r0 in rows]
        vs = [v_ref[pl.ds(r0, C), :] for r0 in rows]
        kbs = [k.astype(BF16) for k in ks]
        kks = [lax.dot_general(kb, kb, dn_t, preferred_element_type=F32) for kb in kbs]
        qk_raws = [lax.dot_general(q.astype(BF16), kb, dn_t, preferred_element_type=F32) for q, kb in zip(qs, kbs)]
        combos = [(s, d) for s in range(chunks_per_iter) for d in (0, 1)]
        terms = []
        for s, d in combos:
            r0, c = rows[s], chunks[s]
            g_c = gcol_ref[0, 0, pl.ds(r0, C), d:d + 1]
            b_c = gcol_ref[0, 0, pl.ds(r0, C), 2 + d:3 + d]
            g_r = grow_ref[0, 0, d, pl.ds(c, 1), :]
            cum_col = jnp.sum(jnp.where(incl[d], g_r, 0.0), axis=1, keepdims=True)
            cum_row = jnp.sum(jnp.where(incl[1 - d], g_c, 0.0), axis=0, keepdims=True)
            cum_last = jnp.sum(g_r, axis=1, keepdims=True)
            decay = jnp.where(incl[d], jnp.exp(jnp.where(incl[d], cum_col - cum_row, 0.0)), 0.0)
            a_mat = jnp.where(strict[d], b_c * kks[s] * decay, 0.0)
            terms.append((b_c, cum_col, cum_last, decay, a_mat))
        t_invs = _unit_lower_inverses([t[4] for t in terms])
        sols = []
        for (s, d), (b_c, cum_col, cum_last, decay, _), t_inv in zip(combos, terms, t_invs):
            e_g = jnp.exp(cum_col)
            rhs = jnp.concatenate([vs[s] * b_c, ks[s] * (b_c * e_g)], axis=1)
            sols.append((e_g, jnp.dot(t_inv.astype(BF16), rhs.astype(BF16), preferred_element_type=F32)))
        for (s, d), (b_c, cum_col, cum_last, decay, _), (e_g, sol) in zip(combos, terms, sols):
            r0, c = rows[s], chunks[s]
            r1 = pl.multiple_of(c * 2 * C, 2 * C)
            u_scr[d, pl.ds(r0, C), :] = sol[:, :GDN_HEAD_DIM]
            wq_scr[d, pl.ds(r1, C), :] = sol[:, GDN_HEAD_DIM:].astype(BF16)
            wq_scr[d, pl.ds(r1 + C, C), :] = (qs[s] * e_g).astype(BF16)
            kd_scr[d, pl.ds(r0, C), :] = (ks[s] * jnp.exp(cum_last - cum_col)).astype(BF16)
            qk_scr[d, pl.ds(r0, C), :] = (qk_raws[s] * decay).astype(BF16)
            cd_scr[d, pl.ds(c, 1), :] = jnp.broadcast_to(jnp.exp(cum_last), (1, GDN_HEAD_DIM))
        return carry

    lax.fori_loop(0, n_chunks // chunks_per_iter, pre_iter, 0)

    o_ref[...] = jnp.zeros_like(o_ref)

    def step(t, states):
        c_bwd = jnp.where(t < n_ctx_chunks, n_ctx_chunks - 1 - t, n_chunks - 1 + n_ctx_chunks - t)
        chunks = (t, c_bwd)
        rows = [pl.multiple_of(c * C, C) for c in chunks]
        ws_qs = [jnp.dot(wq_scr[d, pl.ds(pl.multiple_of(chunks[d] * 2 * C, 2 * C), 2 * C), :],
                         states[d].astype(BF16), preferred_element_type=F32) for d in (0, 1)]
        v_new_b = [(u_scr[d, pl.ds(rows[d], C), :] - ws_qs[d][:C]).astype(BF16) for d in (0, 1)]
        intra = [jnp.dot(qk_scr[d, pl.ds(rows[d], C), :], v_new_b[d], preferred_element_type=F32) for d in (0, 1)]
        upd = [lax.dot_general(kd_scr[d, pl.ds(rows[d], C), :], v_new_b[d], (((0,), (0,)), ((), ())),
                               preferred_element_type=F32) for d in (0, 1)]
        for d in (0, 1):
            o_ref[0, pl.ds(rows[d], C), :] += ws_qs[d][C:] + intra[d]
        return tuple(states[d] * cd_scr[d, pl.ds(chunks[d], 1), :] + upd[d] for d in (0, 1))

    s0 = jnp.zeros((GDN_HEAD_DIM, GDN_HEAD_DIM), F32)
    lax.fori_loop(0, n_chunks, step, (s0, s0))

    for blk in range(Lt // RB):
        r0 = blk * RB
        o = o_ref[0, r0:r0 + RB, :]
        z = pz_ref[0, r0:r0 + RB, :]
        y = o * lax.rsqrt(jnp.mean(o * o, axis=1, keepdims=True) + RMS_EPS) * nw_ref[...]
        o_ref[0, r0:r0 + RB, :] = y * (z * jax.nn.sigmoid(z))


def gated_delta_heads(p, conv_w, gcol, grow, norm_w, *, n_ctx_chunks):
    B, Lt = p.shape[:2]
    H, dk = GDN_HEADS, GDN_HEAD_DIM
    n_chunks = Lt // CHUNK
    chunks_per_iter = GDN_CHUNKS_PER_ITER
    assert n_chunks % chunks_per_iter == 0 and Lt % GDN_PREP_ROWS == 0
    cd_rows = -(-n_chunks // 8) * 8
    col_spec = lambda first: pl.BlockSpec((1, Lt, dk), lambda b, h: (b, 0, first + h))
    w_spec = lambda first: pl.BlockSpec((1, CONV_W, dk), lambda b, h: (first + h, 0, 0))
    scratch = [pltpu.VMEM((Lt + 2 * V7X_SUBLANES, dk), F32),
               pltpu.VMEM((Lt, dk), F32),
               pltpu.VMEM((Lt, dk), F32),
               pltpu.VMEM((Lt, dk), F32),
               pltpu.VMEM((2, Lt, dk), F32),
               pltpu.VMEM((2, 2 * Lt, dk), BF16),
               pltpu.VMEM((2, Lt, dk), BF16),
               pltpu.VMEM((2, Lt, CHUNK), BF16),
               pltpu.VMEM((2, cd_rows, dk), F32)]
    est = (2 * 5 * Lt * dk * 4
           + 2 * Lt * V7X_LANES * 4
           + 4 * Lt * dk * 4
           + 2 * Lt * dk * 4 + 2 * 2 * Lt * dk * 2 + 2 * Lt * dk * 2 + 2 * Lt * V7X_LANES * 2)
    body = functools.partial(_gdn_body, n_chunks=n_chunks, n_ctx_chunks=n_ctx_chunks,
                             chunks_per_iter=chunks_per_iter)
    return pl.pallas_call(
        body,
        grid=(B, H),
        in_specs=[col_spec(0), col_spec(H), col_spec(2 * H), col_spec(3 * H),
                  w_spec(0), w_spec(H), w_spec(2 * H),
                  pl.BlockSpec((1, 1, Lt, 4), lambda b, h: (b, h, 0, 0)),
                  pl.BlockSpec((1, 1, 2, n_chunks, CHUNK), lambda b, h: (b, h, 0, 0, 0)),
                  pl.BlockSpec((1, dk), lambda b, h: (0, 0))],
        out_specs=pl.BlockSpec((1, Lt, dk), lambda b, h: (b, 0, h)),
        out_shape=jax.ShapeDtypeStruct((B, Lt, H * dk), F32),
        scratch_shapes=scratch,
        compiler_params=pltpu.CompilerParams(dimension_semantics=("parallel", "parallel"),
                                             vmem_limit_bytes=int(min(VMEM_CAP, est + est // 8))),
        name="gated_delta_scan",
    )(p, p, p, p, conv_w, conv_w, conv_w, gcol, grow, norm_w)


def _attn_prep_body(a_ref, cos_ref, sin_ref, qw_ref, kw_ref, q_ref, kt_ref, v_ref):
    hd = ATT_HEAD_DIM
    tp = a_ref.shape[1]
    cos = cos_ref[...]
    sin = sin_ref[...]
    lane = lax.broadcasted_iota(jnp.int32, (tp, hd), 1)
    first_half = (lane // (ROPE_AXIS_DIM // 2)) % 2 == 0

    def norm_rope(x, w):
        x = x * lax.rsqrt(jnp.mean(x * x, axis=1, keepdims=True) + RMS_EPS) * w
        partner = jnp.where(first_half, pltpu.roll(x, hd - ROPE_AXIS_DIM // 2, axis=1),
                            pltpu.roll(x, ROPE_AXIS_DIM // 2, axis=1))
        return x * cos + partner * sin

    for h in range(ATT_Q_HEADS):
        q = norm_rope(a_ref[0, :, h * hd:(h + 1) * hd], qw_ref[...])
        q_ref[0, h] = (q * hd ** -0.5).astype(BF16)
    for h in range(ATT_KV_HEADS):
        k = norm_rope(a_ref[0, :, ATT_WIDTH + h * hd:ATT_WIDTH + (h + 1) * hd], kw_ref[...])
        kt_ref[0, h] = k.T.astype(BF16)
        c0 = ATT_WIDTH + ATT_KV_WIDTH + h * hd
        v_ref[0, h] = a_ref[0, :, c0:c0 + hd].astype(BF16)


def attention_inputs(p, cos, sin, q_norm_w, k_norm_w):
    B, Lt, n_cols = p.shape
    hd = ATT_HEAD_DIM
    width = ATT_WIDTH + 2 * ATT_KV_WIDTH
    assert n_cols % width == 0
    tp = ATT_Q_TILE
    est = 2 * (tp * width * 4 + 2 * tp * hd * 4 + tp * width * 2) + 8 * tp * hd * 4
    return pl.pallas_call(
        _attn_prep_body,
        grid=(B, Lt // tp),
        in_specs=[pl.BlockSpec((1, tp, width), lambda b, i: (b, i, n_cols // width - 1)),
                  pl.BlockSpec((tp, hd), lambda b, i: (i, 0)),
                  pl.BlockSpec((tp, hd), lambda b, i: (i, 0)),
                  pl.BlockSpec((1, hd), lambda b, i: (0, 0)),
                  pl.BlockSpec((1, hd), lambda b, i: (0, 0))],
        out_specs=[pl.BlockSpec((1, ATT_Q_HEADS, tp, hd), lambda b, i: (b, 0, i, 0)),
                   pl.BlockSpec((1, ATT_KV_HEADS, hd, tp), lambda b, i: (b, 0, 0, i)),
                   pl.BlockSpec((1, ATT_KV_HEADS, tp, hd), lambda b, i: (b, 0, i, 0))],
        out_shape=[jax.ShapeDtypeStruct((B, ATT_Q_HEADS, Lt, hd), BF16),
                   jax.ShapeDtypeStruct((B, ATT_KV_HEADS, hd, Lt), BF16),
                   jax.ShapeDtypeStruct((B, ATT_KV_HEADS, Lt, hd), BF16)],
        compiler_params=pltpu.CompilerParams(dimension_semantics=("parallel", "parallel"),
                                             vmem_limit_bytes=_vmem_limit(est)),
        name="attention_inputs",
    )(p, cos, sin, q_norm_w.reshape(1, hd), k_norm_w.reshape(1, hd))


def _attn_body(q_ref, kt_ref, v_ref, o_ref, s_scr, *, n_ctx, first_tile):
    G, tq, hd = q_ref.shape[2:]
    rows = G * tq
    tk = s_scr.shape[2]
    tile = pl.program_id(2) + first_tile
    q = q_ref[0, 0].reshape(rows, hd)

    def run(n_blocks):
        m = None
        for j in range(n_blocks):
            s = jnp.dot(q, kt_ref[0, 0, :, j * tk:(j + 1) * tk], preferred_element_type=F32)
            s_scr[j] = s
            bm = jnp.max(s, axis=1, keepdims=True)
            m = bm if m is None else jnp.maximum(m, bm)
        denom = jnp.zeros((rows, 1), F32)
        acc = jnp.zeros((rows, hd), F32)
        for j in range(n_blocks):
            p = jnp.exp(s_scr[j] - m)
            denom = denom + jnp.sum(p, axis=1, keepdims=True)
            acc = acc + jnp.dot(p.astype(BF16), v_ref[0, 0, j * tk:(j + 1) * tk, :], preferred_element_type=F32)
        o = acc / denom
        for g in range(G):
            o_ref[0, :, g * hd:(g + 1) * hd] = o[g * tq:(g + 1) * tq]

    if first_tile * tq < n_ctx:
        pl.when(tile * tq < n_ctx)(lambda: run(n_ctx // tk))
        pl.when(tile * tq >= n_ctx)(lambda: run(v_ref.shape[2] // tk))
    else:
        run(v_ref.shape[2] // tk)


def attention(q, kt, v, *, n_ctx, skip_ctx_queries):
    B, Hkv, G, Lt, hd = q.shape
    tq = ATT_Q_TILE
    assert Lt % tq == 0 and n_ctx % tq == 0
    first_tile = n_ctx // tq if skip_ctx_queries else 0
    n_tiles = Lt // tq - first_tile
    tk = ATT_KV_TILE
    assert Lt % tk == 0 and n_ctx % tk == 0
    rows = G * tq
    scratch = [pltpu.VMEM((Lt // tk, rows, tk), F32)]
    est = 2 * (rows * hd * 2 + 2 * Lt * hd * 2 + tq * G * hd * 4) + rows * Lt * 4 + 8 * rows * tk * 4
    body = functools.partial(_attn_body, n_ctx=n_ctx, first_tile=first_tile)
    return pl.pallas_call(
        body,
        grid=(B, Hkv, n_tiles),
        in_specs=[pl.BlockSpec((1, 1, G, tq, hd), lambda b, h, i: (b, h, 0, i + first_tile, 0)),
                  pl.BlockSpec((1, 1, hd, Lt), lambda b, h, i: (b, h, 0, 0)),
                  pl.BlockSpec((1, 1, Lt, hd), lambda b, h, i: (b, h, 0, 0))],
        out_specs=pl.BlockSpec((1, tq, G * hd), lambda b, h, i: (b, i, h)),
        out_shape=jax.ShapeDtypeStruct((B, n_tiles * tq, Hkv * G * hd), F32),
        scratch_shapes=scratch,
        compiler_params=pltpu.CompilerParams(dimension_semantics=("parallel", "parallel", "parallel"),
                                             vmem_limit_bytes=_vmem_limit(est // 2)),
        name="gqa_attention",
    )(q, kt, v)


def _rows_2d(x_ref):
    tm = x_ref.shape[0] // V7X_SUBLANES
    return jnp.concatenate([x_ref[pl.ds(c, tm, stride=V7X_SUBLANES), :] for c in range(V7X_SUBLANES)], axis=1)


def _router_body(h_ref, w_ref, bias_ref, eidx_ref, wts_ref, rank_ref, counts_ref, carry_scr):
    tm = h_ref.shape[0] // V7X_SUBLANES
    E = N_EXPERTS
    per_group = E // N_GROUPS
    neg = -jnp.inf

    @pl.when(pl.program_id(0) == 0)
    def _():
        carry_scr[...] = jnp.zeros_like(carry_scr)

    logits = jnp.dot(_rows_2d(h_ref).astype(BF16), w_ref[...], preferred_element_type=F32)
    s = jax.nn.sigmoid(logits)
    sel = s + bias_ref[...]
    lane = lax.broadcasted_iota(jnp.int32, (tm, E), 1)
    lane_f = lane.astype(F32)
    grp = lane // per_group
    scores = []
    for g in range(N_GROUPS):
        v = jnp.where(grp == g, sel, neg)
        m1 = jnp.max(v, axis=1, keepdims=True)
        i1 = jnp.min(jnp.where(v == m1, lane_f, float(E)), axis=1, keepdims=True)
        m2 = jnp.max(jnp.where(lane_f == i1, neg, v), axis=1, keepdims=True)
        scores.append(m1 + m2)
    allowed = jnp.zeros((tm, E), jnp.bool_)
    for g in range(N_GROUPS):
        beaten = jnp.zeros((tm, 1), F32)
        for o in range(N_GROUPS):
            if o != g:
                wins = (scores[o] > scores[g]) if o > g else (scores[o] >= scores[g])
                beaten = beaten + wins.astype(F32)
        allowed = jnp.logical_or(allowed, jnp.logical_and(grp == g, beaten < TOPK_GROUPS))
    masked = jnp.where(allowed, sel, neg)
    idx_cols, s_cols = [], []
    chosen = jnp.zeros((tm, E), jnp.bool_)
    for _ in range(TOP_K):
        m = jnp.max(masked, axis=1, keepdims=True)
        idx = jnp.min(jnp.where(masked == m, lane_f, float(E)), axis=1, keepdims=True)
        hit = lane_f == idx
        s_cols.append(jnp.sum(jnp.where(hit, s, 0.0), axis=1, keepdims=True))
        idx_cols.append(idx)
        chosen = jnp.logical_or(chosen, hit)
        masked = jnp.where(hit, neg, masked)
    ri = lax.broadcasted_iota(jnp.int32, (tm, tm), 0)
    ci = lax.broadcasted_iota(jnp.int32, (tm, tm), 1)
    running = jnp.dot((ci < ri).astype(BF16), chosen.astype(BF16), preferred_element_type=F32) + carry_scr[...]
    total = s_cols[0]
    for col in s_cols[1:]:
        total = total + col
    k_lane = lax.broadcasted_iota(jnp.int32, (tm, TOP_K), 1)
    eidx = jnp.zeros((tm, TOP_K), F32)
    wts = jnp.zeros((tm, TOP_K), F32)
    rank = jnp.zeros((tm, TOP_K), F32)
    for j in range(TOP_K):
        r_j = jnp.sum(jnp.where(lane_f == idx_cols[j], running, 0.0), axis=1, keepdims=True)
        eidx = jnp.where(k_lane == j, idx_cols[j], eidx)
        wts = jnp.where(k_lane == j, s_cols[j] / total * ROUTED_SCALE, wts)
        rank = jnp.where(k_lane == j, r_j, rank)
    eidx_ref[...] = eidx.astype(jnp.int32)
    wts_ref[...] = wts
    rank_ref[...] = rank.astype(jnp.int32)
    carry_scr[...] = carry_scr[...] + jnp.sum(chosen.astype(F32), axis=0, keepdims=True)
    counts_ref[...] = carry_scr[...].astype(jnp.int32)


def moe_router(h, router_w_bf, router_bias):
    T = h.shape[0]
    D, E = router_w_bf.shape
    tm = ROUTER_ROWS
    assert T % tm == 0
    tok = pl.BlockSpec((tm, TOP_K), lambda i: (i, 0))
    est = 2 * (tm * D * 4 + D * E * 2) + tm * tm * 4 + 24 * tm * E * 4
    h = h.reshape(T * V7X_SUBLANES, V7X_LANES)
    return pl.pallas_call(
        _router_body,
        grid=(T // tm,),
        in_specs=[pl.BlockSpec((tm * V7X_SUBLANES, V7X_LANES), lambda i: (i, 0)),
                  pl.BlockSpec((D, E), lambda i: (0, 0)),
                  pl.BlockSpec((1, E), lambda i: (0, 0))],
        out_specs=[tok, tok, tok, pl.BlockSpec((1, E), lambda i: (0, 0))],
        out_shape=[jax.ShapeDtypeStruct((T, TOP_K), jnp.int32), jax.ShapeDtypeStruct((T, TOP_K), F32),
                   jax.ShapeDtypeStruct((T, TOP_K), jnp.int32), jax.ShapeDtypeStruct((1, E), jnp.int32)],
        scratch_shapes=[pltpu.VMEM((1, E), F32)],
        compiler_params=pltpu.CompilerParams(dimension_semantics=("arbitrary",),
                                             vmem_limit_bytes=_vmem_limit(est // 2)),
        name="moe_router",
    )(h, router_w_bf, router_bias.reshape(1, E))


def _dispatch_body(pstart_ref, pad_lo_ref, pad_hi_ref, eidx_ref, rank_ref, h_ref, xs_hbm, zero_scr, sems):
    i = pl.program_id(0)
    tt = h_ref.shape[0]

    def row_copy(src, dst_row, sem):
        return pltpu.make_async_copy(src, xs_hbm.at[dst_row], sem)

    @pl.when(i == 0)
    def _():
        zero_scr[...] = jnp.zeros_like(zero_scr)
        for wait in (False, True):
            def per_expert(e, carry, wait=wait):
                def per_row(r, c):
                    cp = row_copy(zero_scr, r, sems.at[1])
                    cp.wait() if wait else cp.start()
                    return c
                return lax.fori_loop(pad_lo_ref[e], pad_hi_ref[e], per_row, carry)
            lax.fori_loop(0, N_EXPERTS, per_expert, 0)

    def issue(p, c):
        toks = [p * DMA_TOKENS_PER_ITER + u for u in range(DMA_TOKENS_PER_ITER)]
        dst = [[pstart_ref[eidx_ref[r * TOP_K + j]] + rank_ref[r * TOP_K + j] for j in range(TOP_K)]
               for r in toks]
        for r, dst_r in zip(toks, dst):
            for d in dst_r:
                row_copy(h_ref.at[r], d, sems.at[0]).start()
        return c
    lax.fori_loop(0, tt // DMA_TOKENS_PER_ITER, issue, 0)

    def drain(r, c):
        for j in range(TOP_K):
            row_copy(h_ref.at[0], 0, sems.at[0]).wait()
        return c
    lax.fori_loop(0, tt, drain, 0)


def moe_dispatch(h, eidx_flat, rank_flat, pad_starts, pad_lo, pad_hi, n_rows):
    T = h.shape[0]
    tile = h.shape[1:]
    tt = DISPATCH_TOKENS
    assert T % tt == 0
    idx_spec = pl.BlockSpec((tt * TOP_K,), lambda i, *_: (i,), memory_space=pltpu.SMEM)
    grid_spec = pltpu.PrefetchScalarGridSpec(
        num_scalar_prefetch=3, grid=(T // tt,),
        in_specs=[idx_spec, idx_spec, pl.BlockSpec((tt,) + tile, lambda i, *_: (i, 0, 0))],
        out_specs=pl.BlockSpec(memory_space=pl.ANY),
        scratch_shapes=[pltpu.VMEM(tile, h.dtype), pltpu.SemaphoreType.DMA((2,))])
    return pl.pallas_call(
        _dispatch_body,
        grid_spec=grid_spec,
        out_shape=jax.ShapeDtypeStruct((n_rows,) + tile, h.dtype),
        compiler_params=pltpu.CompilerParams(dimension_semantics=("arbitrary",)),
        name="moe_dispatch",
    )(pad_starts, pad_lo, pad_hi, eidx_flat, rank_flat, h)


def _combine_body(pstart_ref, eidx_ref, rank_ref, wts_ref, shared_ref, rows_hbm, o_ref, buf, sems, *, tt, n_tiles):
    i = pl.program_id(0)

    def row_copy(src_row, slot, j, r):
        return pltpu.make_async_copy(rows_hbm.at[src_row], buf.at[slot, j, r], sems.at[slot])

    @pl.when(i < n_tiles)
    def _():
        slot = i % 2

        def issue(p, c):
            toks = [p * DMA_TOKENS_PER_ITER + u for u in range(DMA_TOKENS_PER_ITER)]
            src = [[pstart_ref[eidx_ref[r * TOP_K + j]] + rank_ref[r * TOP_K + j] for j in range(TOP_K)]
                   for r in toks]
            for r, src_r in zip(toks, src):
                for j, s in enumerate(src_r):
                    row_copy(s, slot, j, r).start()
            return c
        lax.fori_loop(0, tt // DMA_TOKENS_PER_ITER, issue, 0)

    @pl.when(i >= 1)
    def _():
        slot = (i + 1) % 2

        def drain(r, c):
            for j in range(TOP_K):
                row_copy(0, slot, j, 0).wait()
            return c
        lax.fori_loop(0, tt, drain, 0)

        def per_token(r, c):
            acc = shared_ref[r]
            for j in range(TOP_K):
                acc = acc + buf[slot, j, r] * wts_ref[r * TOP_K + j]
            o_ref[r] = acc
            return c
        lax.fori_loop(0, tt, per_token, 0, unroll=4)


def moe_combine(rows, eidx_flat, rank_flat, pad_starts, wts_flat, shared):
    T = shared.shape[0]
    tile = shared.shape[1:]
    tt = COMBINE_TOKENS
    assert T % tt == 0
    n_tiles = T // tt
    prev = lambda i, *_: (jnp.maximum(i - 1, 0), 0, 0)
    idx_spec = pl.BlockSpec((tt * TOP_K,), lambda i, *_: (jnp.minimum(i, n_tiles - 1),), memory_space=pltpu.SMEM)
    wts_spec = pl.BlockSpec((tt * TOP_K,), lambda i, *_: (jnp.maximum(i - 1, 0),), memory_space=pltpu.SMEM)
    grid_spec = pltpu.PrefetchScalarGridSpec(
        num_scalar_prefetch=1, grid=(n_tiles + 1,),
        in_specs=[idx_spec, idx_spec, wts_spec, pl.BlockSpec((tt,) + tile, prev),
                  pl.BlockSpec(memory_space=pl.ANY)],
        out_specs=pl.BlockSpec((tt,) + tile, prev),
        scratch_shapes=[pltpu.VMEM((2, TOP_K, tt) + tile, F32), pltpu.SemaphoreType.DMA((2,))])
    row_bytes = tile[0] * tile[1] * 4
    est = 2 * TOP_K * tt * row_bytes + 4 * tt * row_bytes
    return pl.pallas_call(
        functools.partial(_combine_body, tt=tt, n_tiles=n_tiles),
        grid_spec=grid_spec,
        out_shape=jax.ShapeDtypeStruct((T,) + tile, F32),
        compiler_params=pltpu.CompilerParams(dimension_semantics=("arbitrary",),
                                             vmem_limit_bytes=_vmem_limit(est)),
        name="moe_combine",
    )(pad_starts, eidx_flat, rank_flat, wts_flat, shared, rows)


def _expert_body(blk_e_ref, n_used_ref, x_ref, wg_ref, wu_ref, wd_ref, o_ref, wg_b, wu_b, wd_b):
    i = pl.program_id(0)

    @pl.when(jnp.logical_or(i == 0, blk_e_ref[i] != blk_e_ref[jnp.maximum(i - 1, 0)]))
    def _():
        wg_b[...] = wg_ref[0, 0].astype(BF16)
        wu_b[...] = wu_ref[0, 0].astype(BF16)
        wd_b[...] = wd_ref[0, 0].astype(BF16)

    @pl.when(i < n_used_ref[0])
    def _():
        x = _rows_2d(x_ref).astype(BF16)
        a = jnp.dot(x, wg_b[...], preferred_element_type=F32)
        b = jnp.dot(x, wu_b[...], preferred_element_type=F32)
        h = (a * jax.nn.sigmoid(a)) * b
        o = jnp.dot(h.astype(BF16), wd_b[...], preferred_element_type=F32)
        tm = o.shape[0]
        for c in range(V7X_SUBLANES):
            o_ref[pl.ds(c, tm, stride=V7X_SUBLANES), :] = o[:, c * V7X_LANES:(c + 1) * V7X_LANES]


def expert_mlp(xs, blk_e, n_used, layer, w_gate, w_up, w_down):
    P = xs.shape[0]
    tile = xs.shape[1:]
    D = tile[0] * tile[1]
    F = w_gate.shape[-1]
    tm = EXPERT_ROWS
    n_blocks = P // tm
    xs = xs.reshape(P * tile[0], tile[1])

    def row_map(i, blk_e_ref, n_used_ref):
        return (jnp.minimum(i, n_used_ref[0] - 1), 0)

    def w_map(i, blk_e_ref, n_used_ref):
        return (layer, blk_e_ref[i], 0, 0)

    est = 2 * (2 * tm * D * 4 + 3 * D * F * 4) + 3 * D * F * 2 + tm * D * 2 + 3 * tm * F * 4
    grid_spec = pltpu.PrefetchScalarGridSpec(
        num_scalar_prefetch=2,
        grid=(n_blocks,),
        in_specs=[pl.BlockSpec((tm * tile[0], tile[1]), row_map),
                  pl.BlockSpec((1, 1, D, F), w_map),
                  pl.BlockSpec((1, 1, D, F), w_map),
                  pl.BlockSpec((1, 1, F, D), w_map)],
        out_specs=pl.BlockSpec((tm * tile[0], tile[1]), row_map),
        scratch_shapes=[pltpu.VMEM((D, F), BF16), pltpu.VMEM((D, F), BF16), pltpu.VMEM((F, D), BF16)],
    )
    out = pl.pallas_call(
        _expert_body,
        grid_spec=grid_spec,
        out_shape=jax.ShapeDtypeStruct(xs.shape, F32),
        compiler_params=pltpu.CompilerParams(dimension_semantics=("arbitrary",),
                                             vmem_limit_bytes=_vmem_limit(est)),
        name="expert_mlp",
    )(blk_e, n_used, xs, w_gate, w_up, w_down)
    return out.reshape((P,) + tile)


def _resnorm_body(*refs, alpha, y_tiles, h_mode):
    x_ref, y_ref, g_ref = refs[:3]
    if h_mode is None:
        w_ref, b_ref, xo_ref = refs[3:]
    else:
        sc_ref, sh_ref, w_ref, b_ref, xo_ref, ho_ref = refs[3:]
    y = _rows_2d(y_ref) if y_tiles else y_ref[0]
    v = alpha * x_ref[0] + g_ref[0] * y
    mu = jnp.mean(v, axis=1, keepdims=True)
    d = v - mu
    xn = d * lax.rsqrt(jnp.mean(d * d, axis=1, keepdims=True) + LN_EPS) * w_ref[...] + b_ref[...]
    xo_ref[0] = xn
    if h_mode is not None:
        h = xn * (1.0 + sc_ref[0]) + sh_ref[0]
        if h_mode == "bf16":
            ho_ref[0] = h.astype(BF16)
        else:
            rows = h.shape[0]
            for c in range(V7X_SUBLANES):
                ho_ref[pl.ds(c, rows, stride=V7X_SUBLANES), :] = h[:, c * V7X_LANES:(c + 1) * V7X_LANES]


def residual_norm(x, y, g_mod, g_chunk, ln_w, ln_b, *, alpha, n_ctx, y_tiles, h_mode=None, h_mod=None,
                  sc_chunk=None, sh_chunk=None):
    B, R, D = x.shape
    tr = RESNORM_ROWS
    assert R % tr == 0 and n_ctx % tr == 0
    n_ctx_tiles = n_ctx // tr
    n_i = R // tr

    def mod_spec(chunk):
        return pl.BlockSpec((1, 1, D), lambda b, i: (jnp.where(i < n_ctx_tiles, B, b), 0, chunk))

    row_spec = pl.BlockSpec((1, tr, D), lambda b, i: (b, i, 0))
    tile_spec = pl.BlockSpec((tr * V7X_SUBLANES, V7X_LANES), lambda b, i: (b * n_i + i, 0))
    vec_spec = pl.BlockSpec((1, D), lambda b, i: (0, 0))
    in_specs = [row_spec, tile_spec if y_tiles else row_spec, mod_spec(g_chunk)]
    args = [x, y if y_tiles else y.reshape(B, R, D), g_mod]
    out_shape = [jax.ShapeDtypeStruct((B, R, D), F32)]
    out_specs = [row_spec]
    if h_mode is not None:
        in_specs += [mod_spec(sc_chunk), mod_spec(sh_chunk)]
        args += [h_mod, h_mod]
        if h_mode == "bf16":
            out_shape.append(jax.ShapeDtypeStruct((B, R, D), BF16))
            out_specs.append(row_spec)
        else:
            out_shape.append(jax.ShapeDtypeStruct((B * R * V7X_SUBLANES, V7X_LANES), F32))
            out_specs.append(tile_spec)
    in_specs += [vec_spec, vec_spec]
    args += [ln_w.reshape(1, D), ln_b.reshape(1, D)]
    est = 2 * 4 * tr * D * 4 + 6 * tr * D * 4
    return pl.pallas_call(
        functools.partial(_resnorm_body, alpha=alpha, y_tiles=y_tiles, h_mode=h_mode),
        grid=(B, n_i),
        in_specs=in_specs,
        out_specs=out_specs,
        out_shape=out_shape,
        compiler_params=pltpu.CompilerParams(dimension_semantics=("parallel", "parallel"),
                                             vmem_limit_bytes=_vmem_limit(est)),
        name="residual_norm",
    )(*args)


def _standardize(x, eps):
    mu = jnp.mean(x, -1, keepdims=True)
    var = jnp.mean(jnp.square(x - mu), -1, keepdims=True)
    return (x - mu) * lax.rsqrt(var + eps)


def _rope_tables(n_ctx, n_lat):
    rows = n_lat // GRID_W
    row = jnp.repeat(jnp.arange(rows, dtype=F32), GRID_W)
    col = jnp.tile(jnp.arange(GRID_W, dtype=F32), rows)
    inv_freq = ROPE_THETA ** (-jnp.arange(0, ROPE_AXIS_DIM, 2, dtype=F32) / ROPE_AXIS_DIM)
    ang = jnp.stack([row[:, None] * inv_freq, col[:, None] * inv_freq], axis=1)
    cos = jnp.concatenate([jnp.ones((n_ctx,) + ang.shape[1:], F32), jnp.cos(ang)], axis=0)
    sin = jnp.concatenate([jnp.zeros((n_ctx,) + ang.shape[1:], F32), jnp.sin(ang)], axis=0)
    cos = jnp.stack([cos, cos], axis=2).reshape(n_ctx + n_lat, ATT_HEAD_DIM)
    sin = jnp.stack([-sin, sin], axis=2).reshape(n_ctx + n_lat, ATT_HEAD_DIM)
    return cos, sin


def _mixer(h_bf, w_main, w_ba, conv_w, a_log, dt_bias, gdn_norm_w, q_norm_w, k_norm_w, w_out_bf, cos, sin,
           *, B, Lt, n_ctx, skip_ctx_queries):
    T = B * Lt
    p = matmul(h_bf, w_main, tm=MM_ROWS, tn=1024)
    p_ba = matmul(h_bf, w_ba, tm=MM_ROWS, tn=V7X_LANES)
    p = p.reshape(B, Lt, -1)
    ba = p_ba[:, :4 * GDN_HEADS].reshape(B, Lt, 2, 2, GDN_HEADS)
    beta = jax.nn.sigmoid(ba[:, :, 0])
    g = -jnp.exp(a_log) * jax.nn.softplus(ba[:, :, 1] + dt_bias)
    gcol = jnp.concatenate([g, beta], axis=2).transpose(0, 3, 1, 2)
    grow = g.transpose(0, 3, 2, 1).reshape(B, GDN_HEADS, 2, Lt // CHUNK, CHUNK)
    conv_heads = conv_w.reshape(CONV_W, 3 * GDN_HEADS, GDN_HEAD_DIM).transpose(1, 0, 2)
    gdn = gated_delta_heads(p, conv_heads, gcol, grow, gdn_norm_w.reshape(1, GDN_HEAD_DIM),
                            n_ctx_chunks=n_ctx // CHUNK)
    qa, kta, va = attention_inputs(p, cos, sin, q_norm_w, k_norm_w)
    qa = qa.reshape(B, ATT_KV_HEADS, ATT_GROUP, Lt, ATT_HEAD_DIM)
    att = attention(qa, kta, va, n_ctx=n_ctx, skip_ctx_queries=skip_ctx_queries)
    if skip_ctx_queries:
        gdn = gdn[:, n_ctx:]
    mix = jnp.concatenate([gdn, att], axis=-1).astype(BF16)
    return matmul(mix.reshape(-1, mix.shape[-1]), w_out_bf, tm=MM_ROWS, tn=1024)


def _moe(h, layer, router_w_bf, router_bias, w_gate, w_up, w_down, sh_gate, sh_up, sh_down):
    T = h.shape[0]
    E = N_EXPERTS
    tm = EXPERT_ROWS
    eidx, wts, rank, counts = moe_router(h, router_w_bf, router_bias)
    counts = counts[0]
    padded = (counts + tm - 1) // tm * tm
    pad_ends = jnp.cumsum(padded)
    pad_starts = pad_ends - padded
    n_blocks = -(-T * TOP_K // tm) + E
    n_used = (pad_ends[-1] // tm).astype(jnp.int32)
    blk_first_row = jnp.minimum(jnp.arange(n_blocks, dtype=jnp.int32), n_used - 1) * tm
    blk_e = jnp.sum((pad_ends[None, :] <= blk_first_row[:, None]).astype(jnp.int32), axis=1)
    blk_e = jnp.minimum(blk_e, E - 1)
    eidx_flat, rank_flat = eidx.reshape(-1), rank.reshape(-1)
    xs = moe_dispatch(h, eidx_flat, rank_flat, pad_starts, pad_starts + counts, pad_ends, n_blocks * tm)
    out_rows = expert_mlp(xs, blk_e, n_used.reshape(1), layer, w_gate, w_up, w_down)
    n_sh = T // tm
    shared = expert_mlp(h, jnp.zeros((n_sh,), jnp.int32), jnp.full((1,), n_sh, jnp.int32), layer,
                        sh_gate[:, None], sh_up[:, None], sh_down[:, None])
    return moe_combine(out_rows, eidx_flat, rank_flat, pad_starts, wts.reshape(-1), shared)


def kernel(x, c, ctx, c_ctx, ada_w, ada_b, w_in, conv_w, gdn_a_log, gdn_dt_bias, gdn_norm_w, q_norm_w, k_norm_w,
           w_out, ln1_w, ln1_b, router_w, router_bias, exp_w_gate, exp_w_up, exp_w_down, sh_w_gate, sh_w_up,
           sh_w_down, ln2_w, ln2_b):
    B, L, D = x.shape
    Lc = ctx.shape[1]
    Lt = Lc + L
    depth = ada_w.shape[0]
    alpha = (2.0 * depth) ** 0.25
    cos, sin = _rope_tables(Lc, L)
    xa = _standardize(jnp.concatenate([ctx, x], axis=1), LN_EPS)
    is_ctx = (jnp.arange(Lt) < Lc)[None, :, None]
    cond = jnp.concatenate([c, c_ctx[None], jnp.zeros((16 - B - 1, D), F32)], axis=0)
    cond = jax.nn.silu(cond)
    SH1, SC1, G1, SH2, SC2, G2 = range(6)
    mods = [(matmul(cond, ada_w[l], tm=16, tn=1024) + ada_b[l]).reshape(16, 1, 6 * D) for l in range(depth)]
    pick = lambda m, i: jnp.where(is_ctx, m[B, :, i * D:(i + 1) * D][None], m[:B, :, i * D:(i + 1) * D])
    h = (xa * (1.0 + pick(mods[0], SC1)) + pick(mods[0], SH1)).astype(BF16)
    for l in range(depth):
        last = l == depth - 1
        w_l = w_in[l]
        w_main = jnp.concatenate([w_l[:, :OFF_BA], w_l[:, OFF_ATT:]], axis=1).astype(BF16)
        w_ba = jnp.pad(w_l[:, OFF_BA:OFF_ATT], ((0, 0), (0, V7X_LANES - 4 * GDN_HEADS))).astype(BF16)
        y = _mixer(h.reshape(B * Lt, D), w_main, w_ba, conv_w[l], gdn_a_log[l], gdn_dt_bias[l], gdn_norm_w[l],
                   q_norm_w[l], k_norm_w[l], w_out[l].astype(BF16), cos, sin, B=B, Lt=Lt, n_ctx=Lc,
                   skip_ctx_queries=last)
        if last:
            xa = xa[:, Lc:]
        rows = xa.shape[1]
        n_ctx = 0 if last else Lc
        xa, h2 = residual_norm(xa, y, mods[l], G1, ln1_w[l], ln1_b[l], alpha=alpha, n_ctx=n_ctx, y_tiles=False,
                               h_mode="tiles", h_mod=mods[l], sc_chunk=SC2, sh_chunk=SH2)
        ff = _moe(h2.reshape(B * rows, V7X_SUBLANES, V7X_LANES), l, router_w[l].astype(BF16), router_bias[l],
                  exp_w_gate, exp_w_up, exp_w_down, sh_w_gate, sh_w_up, sh_w_down)
        ff = ff.reshape(B * rows * V7X_SUBLANES, V7X_LANES)
        if last:
            (xa,) = residual_norm(xa, ff, mods[l], G2, ln2_w[l], ln2_b[l], alpha=alpha, n_ctx=n_ctx, y_tiles=True)
        else:
            xa, h = residual_norm(xa, ff, mods[l], G2, ln2_w[l], ln2_b[l], alpha=alpha, n_ctx=n_ctx, y_tiles=True,
                                  h_mode="bf16", h_mod=mods[l + 1], sc_chunk=SC1, sh_chunk=SH1)
    return xa
```

```python
import functools

import jax
import jax.numpy as jnp
from jax import lax
from jax.experimental import pallas as pl
from jax.experimental.pallas import tpu as pltpu

F32 = jnp.float32
BF16 = jnp.bfloat16

GRID_W = 64
GDN_HEAD_DIM = 128
GDN_HEADS = 4
GDN_WIDTH = GDN_HEADS * GDN_HEAD_DIM
CONV_W = 5
CHUNK = 64
ATT_HEAD_DIM = 128
ATT_Q_HEADS = 4
ATT_KV_HEADS = 2
ATT_GROUP = ATT_Q_HEADS // ATT_KV_HEADS
ATT_WIDTH = ATT_Q_HEADS * ATT_HEAD_DIM
ATT_KV_WIDTH = ATT_KV_HEADS * ATT_HEAD_DIM
ROPE_AXIS_DIM = ATT_HEAD_DIM // 2
ROPE_THETA = 10000.0
OFF_Z = 3 * GDN_WIDTH
OFF_BA = 4 * GDN_WIDTH
OFF_ATT = OFF_BA + 4 * GDN_HEADS
N_EXPERTS = 256
TOP_K = 8
N_GROUPS = 8
TOPK_GROUPS = 4
EXPERT_DIM = 256
ROUTED_SCALE = 2.5
LN_EPS = 1e-5
RMS_EPS = 1e-6

V7X_LANES = 128
V7X_SUBLANES = 8
V7X_VMEM_BYTES = 64 * 1024 * 1024
VMEM_CAP = V7X_VMEM_BYTES - 8 * 1024 * 1024

EXPERT_ROWS = 256
GDN_CHUNKS_PER_ITER = 4
GDN_PREP_ROWS = 256
ROUTER_ROWS = 512
DISPATCH_TOKENS = 256
COMBINE_TOKENS = 128
DMA_TOKENS_PER_ITER = 2
RESNORM_ROWS = 256
ATT_Q_TILE = 256
ATT_KV_TILE = 256
MM_ROWS = 1024


def _vmem_limit(estimate_bytes):
    return int(min(VMEM_CAP, max(16 * 1024 * 1024, 2 * estimate_bytes)))


def _mm_body(a_ref, w_ref, o_ref):
    o_ref[...] = jnp.dot(a_ref[...].astype(BF16), w_ref[...].astype(BF16),
                         preferred_element_type=F32).astype(o_ref.dtype)


def matmul(a, w, *, tm, tn, out_dtype=F32):
    M, K = a.shape
    N = w.shape[1]
    assert M % tm == 0 and N % tn == 0, (M, N, tm, tn)
    est = 2 * (tm * K * a.dtype.itemsize + K * tn * w.dtype.itemsize + tm * tn * 4) + tm * K * 2 + K * tn * 2
    return pl.pallas_call(
        _mm_body,
        grid=(M // tm, N // tn),
        in_specs=[pl.BlockSpec((tm, K), lambda i, j: (i, 0)),
                  pl.BlockSpec((K, tn), lambda i, j: (0, j))],
        out_specs=pl.BlockSpec((tm, tn), lambda i, j: (i, j)),
        out_shape=jax.ShapeDtypeStruct((M, N), out_dtype),
        compiler_params=pltpu.CompilerParams(dimension_semantics=("parallel", "parallel"),
                                             vmem_limit_bytes=_vmem_limit(est)),
        name="proj_matmul",
    )(a, w)


def _unit_lower_inverses(a_list):
    n = a_list[0].shape[0]
    ii = lax.broadcasted_iota(jnp.int32, (n, n), 0)
    jj = lax.broadcasted_iota(jnp.int32, (n, n), 1)
    eye = jnp.where(ii == jj, 1.0, 0.0)
    mm = lambda a, b: jnp.dot(a.astype(BF16), b.astype(BF16), preferred_element_type=F32)
    pows = [-a for a in a_list]
    prods = [eye + p for p in pows]
    for _ in range(n.bit_length() - 2):
        pows = [mm(p, p) for p in pows]
        prods = [pr + mm(pr, p) for pr, p in zip(prods, pows)]
    return prods


def _gdn_body(pq_ref, pk_ref, pv_ref, pz_ref, cwq_ref, cwk_ref, cwv_ref, gcol_ref, grow_ref, nw_ref, o_ref,
              xpad_scr, q_ref, k_ref, v_ref, u_scr, wq_scr, kd_scr, qk_scr, cd_scr,
              *, n_chunks, n_ctx_chunks, chunks_per_iter):
    C = CHUNK
    Lt = n_chunks * C
    n_ctx = n_ctx_chunks * C
    RB = GDN_PREP_ROWS
    PAD = V7X_SUBLANES

    def prep(src_ref, w_ref, dst_ref, unit_rows, scale):
        xpad_scr[0:PAD, :] = jnp.zeros((PAD, GDN_HEAD_DIM), F32)
        xpad_scr[PAD + Lt:2 * PAD + Lt, :] = jnp.zeros((PAD, GDN_HEAD_DIM), F32)
        xpad_scr[PAD:PAD + Lt, :] = src_ref[0]
        for blk in range(Lt // RB):
            r0 = blk * RB
            t = r0 + lax.broadcasted_iota(jnp.int32, (RB, 1), 0)
            seg_lo = jnp.where(t < n_ctx, 0, n_ctx)
            seg_hi = jnp.where(t < n_ctx, n_ctx, Lt)
            acc = jnp.zeros((RB, GDN_HEAD_DIM), F32)
            for i in range(CONV_W):
                off = i - CONV_W // 2
                xs = xpad_scr[PAD + r0 + off:PAD + r0 + off + RB, :]
                if off != 0:
                    xs = jnp.where(jnp.logical_and(t + off >= seg_lo, t + off < seg_hi), xs, 0.0)
                acc = acc + xs * w_ref[0, i:i + 1, :]
            y = acc * jax.nn.sigmoid(acc)
            if unit_rows:
                y = y * (lax.rsqrt(jnp.sum(y * y, axis=1, keepdims=True) + RMS_EPS) * scale)
            dst_ref[r0:r0 + RB, :] = y

    prep(pq_ref, cwq_ref, q_ref, True, GDN_HEAD_DIM ** -0.5)
    prep(pk_ref, cwk_ref, k_ref, True, 1.0)
    prep(pv_ref, cwv_ref, v_ref, False, 1.0)

    dn_t = (((1,), (1,)), ((), ()))
    ii = lax.broadcasted_iota(jnp.int32, (C, C), 0)
    jj = lax.broadcasted_iota(jnp.int32, (C, C), 1)
    incl = (ii >= jj, ii <= jj)
    strict = (ii > jj, ii < jj)

    def pre_iter(it, carry):
        chunks = [it * chunks_per_iter + s for s in range(chunks_per_iter)]
        rows = [pl.multiple_of(c * C, C) for c in chunks]
        qs = [q_ref[pl.ds(r0, C), :] for r0 in rows]
        ks = [k_ref[pl.ds(r0, C), :] for r0 in rows]
        vs = [v_ref[pl.ds(r0, C), :] for r0 in rows]
        kbs = [k.astype(BF16) for k in ks]
        kks = [lax.dot_general(kb, kb, dn_t, preferred_element_type=F32) for kb in kbs]
        qk_raws = [lax.dot_general(q.astype(BF16), kb, dn_t, preferred_element_type=F32) for q, kb in zip(qs, kbs)]
        combos = [(s, d) for s in range(chunks_per_iter) for d in (0, 1)]
        terms = []
        for s, d in combos:
            r0, c = rows[s], chunks[s]
            g_c = gcol_ref[0, 0, pl.ds(r0, C), d:d + 1]
            b_c = gcol_ref[0, 0, pl.ds(r0, C), 2 + d:3 + d]
            g_r = grow_ref[0, 0, d, pl.ds(c, 1), :]
            cum_col = jnp.sum(jnp.where(incl[d], g_r, 0.0), axis=1, keepdims=True)
            cum_row = jnp.sum(jnp.where(incl[1 - d], g_c, 0.0), axis=0, keepdims=True)
            cum_last = jnp.sum(g_r, axis=1, keepdims=True)
            decay = jnp.where(incl[d], jnp.exp(jnp.where(incl[d], cum_col - cum_row, 0.0)), 0.0)
            a_mat = jnp.where(strict[d], b_c * kks[s] * decay, 0.0)
            terms.append((b_c, cum_col, cum_last, decay, a_mat))
        t_invs = _unit_lower_inverses([t[4] for t in terms])
        sols = []
        for (s, d), (b_c, cum_col, cum_last, decay, _), t_inv in zip(combos, terms, t_invs):
            e_g = jnp.exp(cum_col)
            rhs = jnp.concatenate([vs[s] * b_c, ks[s] * (b_c * e_g)], axis=1)
            sols.append((e_g, jnp.dot(t_inv.astype(BF16), rhs.astype(BF16), preferred_element_type=F32)))
        for (s, d), (b_c, cum_col, cum_last, decay, _), (e_g, sol) in zip(combos, terms, sols):
            r0, c = rows[s], chunks[s]
            r1 = pl.multiple_of(c * 2 * C, 2 * C)
            u_scr[d, pl.ds(r0, C), :] = sol[:, :GDN_HEAD_DIM]
            wq_scr[d, pl.ds(r1, C), :] = sol[:, GDN_HEAD_DIM:].astype(BF16)
            wq_scr[d, pl.ds(r1 + C, C), :] = (qs[s] * e_g).astype(BF16)
            kd_scr[d, pl.ds(r0, C), :] = (ks[s] * jnp.exp(cum_last - cum_col)).astype(BF16)
            qk_scr[d, pl.ds(r0, C), :] = (qk_raws[s] * decay).astype(BF16)
            cd_scr[d, pl.ds(c, 1), :] = jnp.broadcast_to(jnp.exp(cum_last), (1, GDN_HEAD_DIM))
        return carry

    lax.fori_loop(0, n_chunks // chunks_per_iter, pre_iter, 0)

    o_ref[...] = jnp.zeros_like(o_ref)

    def step(t, states):
        c_bwd = jnp.where(t < n_ctx_chunks, n_ctx_chunks - 1 - t, n_chunks - 1 + n_ctx_chunks - t)
        chunks = (t, c_bwd)
        rows = [pl.multiple_of(c * C, C) for c in chunks]
        ws_qs = [jnp.dot(wq_scr[d, pl.ds(pl.multiple_of(chunks[d] * 2 * C, 2 * C), 2 * C), :],
                         states[d].astype(BF16), preferred_element_type=F32) for d in (0, 1)]
        v_new_b = [(u_scr[d, pl.ds(rows[d], C), :] - ws_qs[d][:C]).astype(BF16) for d in (0, 1)]
        intra = [jnp.dot(qk_scr[d, pl.ds(rows[d], C), :], v_new_b[d], preferred_element_type=F32) for d in (0, 1)]
        upd = [lax.dot_general(kd_scr[d, pl.ds(rows[d], C), :], v_new_b[d], (((0,), (0,)), ((), ())),
                               preferred_element_type=F32) for d in (0, 1)]
        for d in (0, 1):
            o_ref[0, pl.ds(rows[d], C), :] += ws_qs[d][C:] + intra[d]
        return tuple(states[d] * cd_scr[d, pl.ds(chunks[d], 1), :] + upd[d] for d in (0, 1))

    s0 = jnp.zeros((GDN_HEAD_DIM, GDN_HEAD_DIM), F32)
    lax.fori_loop(0, n_chunks, step, (s0, s0))

    for blk in range(Lt // RB):
        r0 = blk * RB
        o = o_ref[0, r0:r0 + RB, :]
        z = pz_ref[0, r0:r0 + RB, :]
        y = o * lax.rsqrt(jnp.mean(o * o, axis=1, keepdims=True) + RMS_EPS) * nw_ref[...]
        o_ref[0, r0:r0 + RB, :] = y * (z * jax.nn.sigmoid(z))


def gated_delta_heads(p, conv_w, gcol, grow, norm_w, *, n_ctx_chunks):
    B, Lt = p.shape[:2]
    H, dk = GDN_HEADS, GDN_HEAD_DIM
    n_chunks = Lt // CHUNK
    chunks_per_iter = GDN_CHUNKS_PER_ITER
    assert n_chunks % chunks_per_iter == 0 and Lt % GDN_PREP_ROWS == 0
    cd_rows = -(-n_chunks // 8) * 8
    col_spec = lambda first: pl.BlockSpec((1, Lt, dk), lambda b, h: (b, 0, first + h))
    w_spec = lambda first: pl.BlockSpec((1, CONV_W, dk), lambda b, h: (first + h, 0, 0))
    scratch = [pltpu.VMEM((Lt + 2 * V7X_SUBLANES, dk), F32),
               pltpu.VMEM((Lt, dk), F32),
               pltpu.VMEM((Lt, dk), F32),
               pltpu.VMEM((Lt, dk), F32),
               pltpu.VMEM((2, Lt, dk), F32),
               pltpu.VMEM((2, 2 * Lt, dk), BF16),
               pltpu.VMEM((2, Lt, dk), BF16),
               pltpu.VMEM((2, Lt, CHUNK), BF16),
               pltpu.VMEM((2, cd_rows, dk), F32)]
    est = (2 * 5 * Lt * dk * 4
           + 2 * Lt * V7X_LANES * 4
           + 4 * Lt * dk * 4
           + 2 * Lt * dk * 4 + 2 * 2 * Lt * dk * 2 + 2 * Lt * dk * 2 + 2 * Lt * V7X_LANES * 2)
    body = functools.partial(_gdn_body, n_chunks=n_chunks, n_ctx_chunks=n_ctx_chunks,
                             chunks_per_iter=chunks_per_iter)
    return pl.pallas_call(
        body,
        grid=(B, H),
        in_specs=[col_spec(0), col_spec(H), col_spec(2 * H), col_spec(3 * H),
                  w_spec(0), w_spec(H), w_spec(2 * H),
                  pl.BlockSpec((1, 1, Lt, 4), lambda b, h: (b, h, 0, 0)),
                  pl.BlockSpec((1, 1, 2, n_chunks, CHUNK), lambda b, h: (b, h, 0, 0, 0)),
                  pl.BlockSpec((1, dk), lambda b, h: (0, 0))],
        out_specs=pl.BlockSpec((1, Lt, dk), lambda b, h: (b, 0, h)),
        out_shape=jax.ShapeDtypeStruct((B, Lt, H * dk), F32),
        scratch_shapes=scratch,
        compiler_params=pltpu.CompilerParams(dimension_semantics=("parallel", "parallel"),
                                             vmem_limit_bytes=int(min(VMEM_CAP, est + est // 8))),
        name="gated_delta_scan",
    )(p, p, p, p, conv_w, conv_w, conv_w, gcol, grow, norm_w)


def _attn_prep_body(a_ref, cos_ref, sin_ref, qw_ref, kw_ref, q_ref, kt_ref, v_ref):
    hd = ATT_HEAD_DIM
    tp = a_ref.shape[1]
    cos = cos_ref[...]
    sin = sin_ref[...]
    lane = lax.broadcasted_iota(jnp.int32, (tp, hd), 1)
    first_half = (lane // (ROPE_AXIS_DIM // 2)) % 2 == 0

    def norm_rope(x, w):
        x = x * lax.rsqrt(jnp.mean(x * x, axis=1, keepdims=True) + RMS_EPS) * w
        partner = jnp.where(first_half, pltpu.roll(x, hd - ROPE_AXIS_DIM // 2, axis=1),
                            pltpu.roll(x, ROPE_AXIS_DIM // 2, axis=1))
        return x * cos + partner * sin

    for h in range(ATT_Q_HEADS):
        q = norm_rope(a_ref[0, :, h * hd:(h + 1) * hd], qw_ref[...])
        q_ref[0, h] = (q * hd ** -0.5).astype(BF16)
    for h in range(ATT_KV_HEADS):
        k = norm_rope(a_ref[0, :, ATT_WIDTH + h * hd:ATT_WIDTH + (h + 1) * hd], kw_ref[...])
        kt_ref[0, h] = k.T.astype(BF16)
        c0 = ATT_WIDTH + ATT_KV_WIDTH + h * hd
        v_ref[0, h] = a_ref[0, :, c0:c0 + hd].astype(BF16)


def attention_inputs(p, cos, sin, q_norm_w, k_norm_w):
    B, Lt, n_cols = p.shape
    hd = ATT_HEAD_DIM
    width = ATT_WIDTH + 2 * ATT_KV_WIDTH
    assert n_cols % width == 0
    tp = ATT_Q_TILE
    est = 2 * (tp * width * 4 + 2 * tp * hd * 4 + tp * width * 2) + 8 * tp * hd * 4
    return pl.pallas_call(
        _attn_prep_body,
        grid=(B, Lt // tp),
        in_specs=[pl.BlockSpec((1, tp, width), lambda b, i: (b, i, n_cols // width - 1)),
                  pl.BlockSpec((tp, hd), lambda b, i: (i, 0)),
                  pl.BlockSpec((tp, hd), lambda b, i: (i, 0)),
                  pl.BlockSpec((1, hd), lambda b, i: (0, 0)),
                  pl.BlockSpec((1, hd), lambda b, i: (0, 0))],
        out_specs=[pl.BlockSpec((1, ATT_Q_HEADS, tp, hd), lambda b, i: (b, 0, i, 0)),
                   pl.BlockSpec((1, ATT_KV_HEADS, hd, tp), lambda b, i: (b, 0, 0, i)),
                   pl.BlockSpec((1, ATT_KV_HEADS, tp, hd), lambda b, i: (b, 0, i, 0))],
        out_shape=[jax.ShapeDtypeStruct((B, ATT_Q_HEADS, Lt, hd), BF16),
                   jax.ShapeDtypeStruct((B, ATT_KV_HEADS, hd, Lt), BF16),
                   jax.ShapeDtypeStruct((B, ATT_KV_HEADS, Lt, hd), BF16)],
        compiler_params=pltpu.CompilerParams(dimension_semantics=("parallel", "parallel"),
                                             vmem_limit_bytes=_vmem_limit(est)),
        name="attention_inputs",
    )(p, cos, sin, q_norm_w.reshape(1, hd), k_norm_w.reshape(1, hd))


def _attn_body(q_ref, kt_ref, v_ref, o_ref, s_scr, *, n_ctx, first_tile):
    G, tq, hd = q_ref.shape[2:]
    rows = G * tq
    tk = s_scr.shape[2]
    tile = pl.program_id(2) + first_tile
    q = q_ref[0, 0].reshape(rows, hd)

    def run(n_blocks):
        m = None
        for j in range(n_blocks):
            s = jnp.dot(q, kt_ref[0, 0, :, j * tk:(j + 1) * tk], preferred_element_type=F32)
            s_scr[j] = s
            bm = jnp.max(s, axis=1, keepdims=True)
            m = bm if m is None else jnp.maximum(m, bm)
        denom = jnp.zeros((rows, 1), F32)
        acc = jnp.zeros((rows, hd), F32)
        for j in range(n_blocks):
            p = jnp.exp(s_scr[j] - m)
            denom = denom + jnp.sum(p, axis=1, keepdims=True)
            acc = acc + jnp.dot(p.astype(BF16), v_ref[0, 0, j * tk:(j + 1) * tk, :], preferred_element_type=F32)
        o = acc / denom
        for g in range(G):
            o_ref[0, :, g * hd:(g + 1) * hd] = o[g * tq:(g + 1) * tq]

    if first_tile * tq < n_ctx:
        pl.when(tile * tq < n_ctx)(lambda: run(n_ctx // tk))
        pl.when(tile * tq >= n_ctx)(lambda: run(v_ref.shape[2] // tk))
    else:
        run(v_ref.shape[2] // tk)


def attention(q, kt, v, *, n_ctx, skip_ctx_queries):
    B, Hkv, G, Lt, hd = q.shape
    tq = ATT_Q_TILE
    assert Lt % tq == 0 and n_ctx % tq == 0
    first_tile = n_ctx // tq if skip_ctx_queries else 0
    n_tiles = Lt // tq - first_tile
    tk = ATT_KV_TILE
    assert Lt % tk == 0 and n_ctx % tk == 0
    rows = G * tq
    scratch = [pltpu.VMEM((Lt // tk, rows, tk), F32)]
    est = 2 * (rows * hd * 2 + 2 * Lt * hd * 2 + tq * G * hd * 4) + rows * Lt * 4 + 8 * rows * tk * 4
    body = functools.partial(_attn_body, n_ctx=n_ctx, first_tile=first_tile)
    return pl.pallas_call(
        body,
        grid=(B, Hkv, n_tiles),
        in_specs=[pl.BlockSpec((1, 1, G, tq, hd), lambda b, h, i: (b, h, 0, i + first_tile, 0)),
                  pl.BlockSpec((1, 1, hd, Lt), lambda b, h, i: (b, h, 0, 0)),
                  pl.BlockSpec((1, 1, Lt, hd), lambda b, h, i: (b, h, 0, 0))],
        out_specs=pl.BlockSpec((1, tq, G * hd), lambda b, h, i: (b, i, h)),
        out_shape=jax.ShapeDtypeStruct((B, n_tiles * tq, Hkv * G * hd), F32),
        scratch_shapes=scratch,
        compiler_params=pltpu.CompilerParams(dimension_semantics=("parallel", "parallel", "parallel"),
                                             vmem_limit_bytes=_vmem_limit(est // 2)),
        name="gqa_attention",
    )(q, kt, v)


def _rows_2d(x_ref):
    tm = x_ref.shape[0] // V7X_SUBLANES
    return jnp.concatenate([x_ref[pl.ds(c, tm, stride=V7X_SUBLANES), :] for c in range(V7X_SUBLANES)], axis=1)


def _router_body(h_ref, w_ref, bias_ref, eidx_ref, wts_ref, rank_ref, counts_ref, carry_scr):
    tm = h_ref.shape[0] // V7X_SUBLANES
    E = N_EXPERTS
    per_group = E // N_GROUPS
    neg = -jnp.inf

    @pl.when(pl.program_id(0) == 0)
    def _():
        carry_scr[...] = jnp.zeros_like(carry_scr)

    logits = jnp.dot(_rows_2d(h_ref).astype(BF16), w_ref[...], preferred_element_type=F32)
    s = jax.nn.sigmoid(logits)
    sel = s + bias_ref[...]
    lane = lax.broadcasted_iota(jnp.int32, (tm, E), 1)
    lane_f = lane.astype(F32)
    grp = lane // per_group
    scores = []
    for g in range(N_GROUPS):
        v = jnp.where(grp == g, sel, neg)
        m1 = jnp.max(v, axis=1, keepdims=True)
        i1 = jnp.min(jnp.where(v == m1, lane_f, float(E)), axis=1, keepdims=True)
        m2 = jnp.max(jnp.where(lane_f == i1, neg, v), axis=1, keepdims=True)
        scores.append(m1 + m2)
    allowed = jnp.zeros((tm, E), jnp.bool_)
    for g in range(N_GROUPS):
        beaten = jnp.zeros((tm, 1), F32)
        for o in range(N_GROUPS):
            if o != g:
                wins = (scores[o] > scores[g]) if o > g else (scores[o] >= scores[g])
                beaten = beaten + wins.astype(F32)
        allowed = jnp.logical_or(allowed, jnp.logical_and(grp == g, beaten < TOPK_GROUPS))
    masked = jnp.where(allowed, sel, neg)
    idx_cols, s_cols = [], []
    chosen = jnp.zeros((tm, E), jnp.bool_)
    for _ in range(TOP_K):
        m = jnp.max(masked, axis=1, keepdims=True)
        idx = jnp.min(jnp.where(masked == m, lane_f, float(E)), axis=1, keepdims=True)
        hit = lane_f == idx
        s_cols.append(jnp.sum(jnp.where(hit, s, 0.0), axis=1, keepdims=True))
        idx_cols.append(idx)
        chosen = jnp.logical_or(chosen, hit)
        masked = jnp.where(hit, neg, masked)
    ri = lax.broadcasted_iota(jnp.int32, (tm, tm), 0)
    ci = lax.broadcasted_iota(jnp.int32, (tm, tm), 1)
    running = jnp.dot((ci < ri).astype(BF16), chosen.astype(BF16), preferred_element_type=F32) + carry_scr[...]
    total = s_cols[0]
    for col in s_cols[1:]:
        total = total + col
    k_lane = lax.broadcasted_iota(jnp.int32, (tm, TOP_K), 1)
    eidx = jnp.zeros((tm, TOP_K), F32)
    wts = jnp.zeros((tm, TOP_K), F32)
    rank = jnp.zeros((tm, TOP_K), F32)
    for j in range(TOP_K):
        r_j = jnp.sum(jnp.where(lane_f == idx_cols[j], running, 0.0), axis=1, keepdims=True)
        eidx = jnp.where(k_lane == j, idx_cols[j], eidx)
        wts = jnp.where(k_lane == j, s_cols[j] / total * ROUTED_SCALE, wts)
        rank = jnp.where(k_lane == j, r_j, rank)
    eidx_ref[...] = eidx.astype(jnp.int32)
    wts_ref[...] = wts
    rank_ref[...] = rank.astype(jnp.int32)
    carry_scr[...] = carry_scr[...] + jnp.sum(chosen.astype(F32), axis=0, keepdims=True)
    counts_ref[...] = carry_scr[...].astype(jnp.int32)


def moe_router(h, router_w_bf, router_bias):
    T = h.shape[0]
    D, E = router_w_bf.shape
    tm = ROUTER_ROWS
    assert T % tm == 0
    tok = pl.BlockSpec((tm, TOP_K), lambda i: (i, 0))
    est = 2 * (tm * D * 4 + D * E * 2) + tm * tm * 4 + 24 * tm * E * 4
    h = h.reshape(T * V7X_SUBLANES, V7X_LANES)
    return pl.pallas_call(
        _router_body,
        grid=(T // tm,),
        in_specs=[pl.BlockSpec((tm * V7X_SUBLANES, V7X_LANES), lambda i: (i, 0)),
                  pl.BlockSpec((D, E), lambda i: (0, 0)),
                  pl.BlockSpec((1, E), lambda i: (0, 0))],
        out_specs=[tok, tok, tok, pl.BlockSpec((1, E), lambda i: (0, 0))],
        out_shape=[jax.ShapeDtypeStruct((T, TOP_K), jnp.int32), jax.ShapeDtypeStruct((T, TOP_K), F32),
                   jax.ShapeDtypeStruct((T, TOP_K), jnp.int32), jax.ShapeDtypeStruct((1, E), jnp.int32)],
        scratch_shapes=[pltpu.VMEM((1, E), F32)],
        compiler_params=pltpu.CompilerParams(dimension_semantics=("arbitrary",),
                                             vmem_limit_bytes=_vmem_limit(est // 2)),
        name="moe_router",
    )(h, router_w_bf, router_bias.reshape(1, E))


def _dispatch_body(pstart_ref, pad_lo_ref, pad_hi_ref, eidx_ref, rank_ref, h_ref, xs_hbm, zero_scr, sems):
    i = pl.program_id(0)
    tt = h_ref.shape[0]

    def row_copy(src, dst_row, sem):
        return pltpu.make_async_copy(src, xs_hbm.at[dst_row], sem)

    @pl.when(i == 0)
    def _():
        zero_scr[...] = jnp.zeros_like(zero_scr)
        for wait in (False, True):
            def per_expert(e, carry, wait=wait):
                def per_row(r, c):
                    cp = row_copy(zero_scr, r, sems.at[1])
                    cp.wait() if wait else cp.start()
                    return c
                return lax.fori_loop(pad_lo_ref[e], pad_hi_ref[e], per_row, carry)
            lax.fori_loop(0, N_EXPERTS, per_expert, 0)

    def issue(p, c):
        toks = [p * DMA_TOKENS_PER_ITER + u for u in range(DMA_TOKENS_PER_ITER)]
        dst = [[pstart_ref[eidx_ref[r * TOP_K + j]] + rank_ref[r * TOP_K + j] for j in range(TOP_K)]
               for r in toks]
        for r, dst_r in zip(toks, dst):
            for j, d in enumerate(dst_r):
                row_copy(h_ref.at[r], d, sems.at[0]).start(priority=j % 2)
        return c
    lax.fori_loop(0, tt // DMA_TOKENS_PER_ITER, issue, 0)

    def drain(r, c):
        for j in range(TOP_K):
            row_copy(h_ref.at[0], 0, sems.at[0]).wait()
        return c
    lax.fori_loop(0, tt, drain, 0)


def moe_dispatch(h, eidx_flat, rank_flat, pad_starts, pad_lo, pad_hi, n_rows):
    T = h.shape[0]
    tile = h.shape[1:]
    tt = DISPATCH_TOKENS
    assert T % tt == 0
    idx_spec = pl.BlockSpec((tt * TOP_K,), lambda i, *_: (i,), memory_space=pltpu.SMEM)
    grid_spec = pltpu.PrefetchScalarGridSpec(
        num_scalar_prefetch=3, grid=(T // tt,),
        in_specs=[idx_spec, idx_spec, pl.BlockSpec((tt,) + tile, lambda i, *_: (i, 0, 0))],
        out_specs=pl.BlockSpec(memory_space=pl.ANY),
        scratch_shapes=[pltpu.VMEM(tile, h.dtype), pltpu.SemaphoreType.DMA((2,))])
    return pl.pallas_call(
        _dispatch_body,
        grid_spec=grid_spec,
        out_shape=jax.ShapeDtypeStruct((n_rows,) + tile, h.dtype),
        compiler_params=pltpu.CompilerParams(dimension_semantics=("arbitrary",)),
        name="moe_dispatch",
    )(pad_starts, pad_lo, pad_hi, eidx_flat, rank_flat, h)


def _combine_body(pstart_ref, eidx_ref, rank_ref, wts_ref, shared_ref, rows_hbm, o_ref, buf, sems, *, tt, n_tiles):
    i = pl.program_id(0)

    def row_copy(src_row, slot, j, r):
        return pltpu.make_async_copy(rows_hbm.at[src_row], buf.at[slot, j, r], sems.at[slot])

    @pl.when(i < n_tiles)
    def _():
        slot = i % 2

        def issue(p, c):
            toks = [p * DMA_TOKENS_PER_ITER + u for u in range(DMA_TOKENS_PER_ITER)]
            src = [[pstart_ref[eidx_ref[r * TOP_K + j]] + rank_ref[r * TOP_K + j] for j in range(TOP_K)]
                   for r in toks]
            for r, src_r in zip(toks, src):
                for j, s in enumerate(src_r):
                    row_copy(s, slot, j, r).start(priority=j % 2)
            return c
        lax.fori_loop(0, tt // DMA_TOKENS_PER_ITER, issue, 0)

    @pl.when(i >= 1)
    def _():
        slot = (i + 1) % 2

        def drain(r, c):
            for j in range(TOP_K):
                row_copy(0, slot, j, 0).wait()
            return c
        lax.fori_loop(0, tt, drain, 0)

        def per_token(r, c):
            acc = shared_ref[r]
            for j in range(TOP_K):
                acc = acc + buf[slot, j, r] * wts_ref[r * TOP_K + j]
            o_ref[r] = acc
            return c
        lax.fori_loop(0, tt, per_token, 0, unroll=4)


def moe_combine(rows, eidx_flat, rank_flat, pad_starts, wts_flat, shared):
    T = shared.shape[0]
    tile = shared.shape[1:]
    tt = COMBINE_TOKENS
    assert T % tt == 0
    n_tiles = T // tt
    prev = lambda i, *_: (jnp.maximum(i - 1, 0), 0, 0)
    idx_spec = pl.BlockSpec((tt * TOP_K,), lambda i, *_: (jnp.minimum(i, n_tiles - 1),), memory_space=pltpu.SMEM)
    wts_spec = pl.BlockSpec((tt * TOP_K,), lambda i, *_: (jnp.maximum(i - 1, 0),), memory_space=pltpu.SMEM)
    grid_spec = pltpu.PrefetchScalarGridSpec(
        num_scalar_prefetch=1, grid=(n_tiles + 1,),
        in_specs=[idx_spec, idx_spec, wts_spec, pl.BlockSpec((tt,) + tile, prev),
                  pl.BlockSpec(memory_space=pl.ANY)],
        out_specs=pl.BlockSpec((tt,) + tile, prev),
        scratch_shapes=[pltpu.VMEM((2, TOP_K, tt) + tile, F32), pltpu.SemaphoreType.DMA((2,))])
    row_bytes = tile[0] * tile[1] * 4
    est = 2 * TOP_K * tt * row_bytes + 4 * tt * row_bytes
    return pl.pallas_call(
        functools.partial(_combine_body, tt=tt, n_tiles=n_tiles),
        grid_spec=grid_spec,
        out_shape=jax.ShapeDtypeStruct((T,) + tile, F32),
        compiler_params=pltpu.CompilerParams(dimension_semantics=("arbitrary",),
                                             vmem_limit_bytes=_vmem_limit(est)),
        name="moe_combine",
    )(pad_starts, eidx_flat, rank_flat, wts_flat, shared, rows)


def _expert_body(blk_e_ref, n_used_ref, x_ref, wg_ref, wu_ref, wd_ref, o_ref, wg_b, wu_b, wd_b):
    i = pl.program_id(0)

    @pl.when(jnp.logical_or(i == 0, blk_e_ref[i] != blk_e_ref[jnp.maximum(i - 1, 0)]))
    def _():
        wg_b[...] = wg_ref[0, 0].astype(BF16)
        wu_b[...] = wu_ref[0, 0].astype(BF16)
        wd_b[...] = wd_ref[0, 0].astype(BF16)

    @pl.when(i < n_used_ref[0])
    def _():
        x = _rows_2d(x_ref).astype(BF16)
        a = jnp.dot(x, wg_b[...], preferred_element_type=F32)
        b = jnp.dot(x, wu_b[...], preferred_element_type=F32)
        h = (a * jax.nn.sigmoid(a)) * b
        o = jnp.dot(h.astype(BF16), wd_b[...], preferred_element_type=F32)
        tm = o.shape[0]
        for c in range(V7X_SUBLANES):
            o_ref[pl.ds(c, tm, stride=V7X_SUBLANES), :] = o[:, c * V7X_LANES:(c + 1) * V7X_LANES]


def expert_mlp(xs, blk_e, n_used, layer, w_gate, w_up, w_down):
    P = xs.shape[0]
    tile = xs.shape[1:]
    D = tile[0] * tile[1]
    F = w_gate.shape[-1]
    tm = EXPERT_ROWS
    n_blocks = P // tm
    xs = xs.reshape(P * tile[0], tile[1])

    def row_map(i, blk_e_ref, n_used_ref):
        return (jnp.minimum(i, n_used_ref[0] - 1), 0)

    def w_map(i, blk_e_ref, n_used_ref):
        return (layer, blk_e_ref[i], 0, 0)

    est = 2 * (2 * tm * D * 4 + 3 * D * F * 4) + 3 * D * F * 2 + tm * D * 2 + 3 * tm * F * 4
    grid_spec = pltpu.PrefetchScalarGridSpec(
        num_scalar_prefetch=2,
        grid=(n_blocks,),
        in_specs=[pl.BlockSpec((tm * tile[0], tile[1]), row_map),
                  pl.BlockSpec((1, 1, D, F), w_map),
                  pl.BlockSpec((1, 1, D, F), w_map),
                  pl.BlockSpec((1, 1, F, D), w_map)],
        out_specs=pl.BlockSpec((tm * tile[0], tile[1]), row_map),
        scratch_shapes=[pltpu.VMEM((D, F), BF16), pltpu.VMEM((D, F), BF16), pltpu.VMEM((F, D), BF16)],
    )
    out = pl.pallas_call(
        _expert_body,
        grid_spec=grid_spec,
        out_shape=jax.ShapeDtypeStruct(xs.shape, F32),
        compiler_params=pltpu.CompilerParams(dimension_semantics=("arbitrary",),
                                             vmem_limit_bytes=_vmem_limit(est)),
        name="expert_mlp",
    )(blk_e, n_used, xs, w_gate, w_up, w_down)
    return out.reshape((P,) + tile)


def _resnorm_body(*refs, alpha, y_tiles, h_mode):
    x_ref, y_ref, g_ref = refs[:3]
    if h_mode is None:
        w_ref, b_ref, xo_ref = refs[3:]
    else:
        sc_ref, sh_ref, w_ref, b_ref, xo_ref, ho_ref = refs[3:]
    y = _rows_2d(y_ref) if y_tiles else y_ref[0]
    v = alpha * x_ref[0] + g_ref[0] * y
    mu = jnp.mean(v, axis=1, keepdims=True)
    d = v - mu
    xn = d * lax.rsqrt(jnp.mean(d * d, axis=1, keepdims=True) + LN_EPS) * w_ref[...] + b_ref[...]
    xo_ref[0] = xn
    if h_mode is not None:
        h = xn * (1.0 + sc_ref[0]) + sh_ref[0]
        if h_mode == "bf16":
            ho_ref[0] = h.astype(BF16)
        else:
            rows = h.shape[0]
            for c in range(V7X_SUBLANES):
                ho_ref[pl.ds(c, rows, stride=V7X_SUBLANES), :] = h[:, c * V7X_LANES:(c + 1) * V7X_LANES]


def residual_norm(x, y, g_mod, g_chunk, ln_w, ln_b, *, alpha, n_ctx, y_tiles, h_mode=None, h_mod=None,
                  sc_chunk=None, sh_chunk=None):
    B, R, D = x.shape
    tr = RESNORM_ROWS
    assert R % tr == 0 and n_ctx % tr == 0
    n_ctx_tiles = n_ctx // tr
    n_i = R // tr

    def mod_spec(chunk):
        return pl.BlockSpec((1, 1, D), lambda b, i: (jnp.where(i < n_ctx_tiles, B, b), 0, chunk))

    row_spec = pl.BlockSpec((1, tr, D), lambda b, i: (b, i, 0))
    tile_spec = pl.BlockSpec((tr * V7X_SUBLANES, V7X_LANES), lambda b, i: (b * n_i + i, 0))
    vec_spec = pl.BlockSpec((1, D), lambda b, i: (0, 0))
    in_specs = [row_spec, tile_spec if y_tiles else row_spec, mod_spec(g_chunk)]
    args = [x, y if y_tiles else y.reshape(B, R, D), g_mod]
    out_shape = [jax.ShapeDtypeStruct((B, R, D), F32)]
    out_specs = [row_spec]
    if h_mode is not None:
        in_specs += [mod_spec(sc_chunk), mod_spec(sh_chunk)]
        args += [h_mod, h_mod]
        if h_mode == "bf16":
            out_shape.append(jax.ShapeDtypeStruct((B, R, D), BF16))
            out_specs.append(row_spec)
        else:
            out_shape.append(jax.ShapeDtypeStruct((B * R * V7X_SUBLANES, V7X_LANES), F32))
            out_specs.append(tile_spec)
    in_specs += [vec_spec, vec_spec]
    args += [ln_w.reshape(1, D), ln_b.reshape(1, D)]
    est = 2 * 4 * tr * D * 4 + 6 * tr * D * 4
    return pl.pallas_call(
        functools.partial(_resnorm_body, alpha=alpha, y_tiles=y_tiles, h_mode=h_mode),
        grid=(B, n_i),
        in_specs=in_specs,
        out_specs=out_specs,
        out_shape=out_shape,
        compiler_params=pltpu.CompilerParams(dimension_semantics=("parallel", "parallel"),
                                             vmem_limit_bytes=_vmem_limit(est)),
        name="residual_norm",
    )(*args)


def _standardize(x, eps):
    mu = jnp.mean(x, -1, keepdims=True)
    var = jnp.mean(jnp.square(x - mu), -1, keepdims=True)
    return (x - mu) * lax.rsqrt(var + eps)


def _rope_tables(n_ctx, n_lat):
    rows = n_lat // GRID_W
    row = jnp.repeat(jnp.arange(rows, dtype=F32), GRID_W)
    col = jnp.tile(jnp.arange(GRID_W, dtype=F32), rows)
    inv_freq = ROPE_THETA ** (-jnp.arange(0, ROPE_AXIS_DIM, 2, dtype=F32) / ROPE_AXIS_DIM)
    ang = jnp.stack([row[:, None] * inv_freq, col[:, None] * inv_freq], axis=1)
    cos = jnp.concatenate([jnp.ones((n_ctx,) + ang.shape[1:], F32), jnp.cos(ang)], axis=0)
    sin = jnp.concatenate([jnp.zeros((n_ctx,) + ang.shape[1:], F32), jnp.sin(ang)], axis=0)
    cos = jnp.stack([cos, cos], axis=2).reshape(n_ctx + n_lat, ATT_HEAD_DIM)
    sin = jnp.stack([-sin, sin], axis=2).reshape(n_ctx + n_lat, ATT_HEAD_DIM)
    return cos, sin


def _mixer(h_bf, w_main, w_ba, conv_w, a_log, dt_bias, gdn_norm_w, q_norm_w, k_norm_w, w_out_bf, cos, sin,
           *, B, Lt, n_ctx, skip_ctx_queries):
    T = B * Lt
    p = matmul(h_bf, w_main, tm=MM_ROWS, tn=1024)
    p_ba = matmul(h_bf, w_ba, tm=MM_ROWS, tn=V7X_LANES)
    p = p.reshape(B, Lt, -1)
    ba = p_ba[:, :4 * GDN_HEADS].reshape(B, Lt, 2, 2, GDN_HEADS)
    beta = jax.nn.sigmoid(ba[:, :, 0])
    g = -jnp.exp(a_log) * jax.nn.softplus(ba[:, :, 1] + dt_bias)
    gcol = jnp.concatenate([g, beta], axis=2).transpose(0, 3, 1, 2)
    grow = g.transpose(0, 3, 2, 1).reshape(B, GDN_HEADS, 2, Lt // CHUNK, CHUNK)
    conv_heads = conv_w.reshape(CONV_W, 3 * GDN_HEADS, GDN_HEAD_DIM).transpose(1, 0, 2)
    gdn = gated_delta_heads(p, conv_heads, gcol, grow, gdn_norm_w.reshape(1, GDN_HEAD_DIM),
                            n_ctx_chunks=n_ctx // CHUNK)
    qa, kta, va = attention_inputs(p, cos, sin, q_norm_w, k_norm_w)
    qa = qa.reshape(B, ATT_KV_HEADS, ATT_GROUP, Lt, ATT_HEAD_DIM)
    att = attention(qa, kta, va, n_ctx=n_ctx, skip_ctx_queries=skip_ctx_queries)
    if skip_ctx_queries:
        gdn = gdn[:, n_ctx:]
    mix = jnp.concatenate([gdn, att], axis=-1).astype(BF16)
    return matmul(mix.reshape(-1, mix.shape[-1]), w_out_bf, tm=MM_ROWS, tn=1024)


def _moe(h, layer, router_w_bf, router_bias, w_gate, w_up, w_down, sh_gate, sh_up, sh_down):
    T = h.shape[0]
    E = N_EXPERTS
    tm = EXPERT_ROWS
    eidx, wts, rank, counts = moe_router(h, router_w_bf, router_bias)
    counts = counts[0]
    padded = (counts + tm - 1) // tm * tm
    pad_ends = jnp.cumsum(padded)
    pad_starts = pad_ends - padded
    n_blocks = -(-T * TOP_K // tm) + E
    n_used = (pad_ends[-1] // tm).astype(jnp.int32)
    blk_first_row = jnp.minimum(jnp.arange(n_blocks, dtype=jnp.int32), n_used - 1) * tm
    blk_e = jnp.sum((pad_ends[None, :] <= blk_first_row[:, None]).astype(jnp.int32), axis=1)
    blk_e = jnp.minimum(blk_e, E - 1)
    eidx_flat, rank_flat = eidx.reshape(-1), rank.reshape(-1)
    xs = moe_dispatch(h, eidx_flat, rank_flat, pad_starts, pad_starts + counts, pad_ends, n_blocks * tm)
    out_rows = expert_mlp(xs, blk_e, n_used.reshape(1), layer, w_gate, w_up, w_down)
    n_sh = T // tm
    shared = expert_mlp(h, jnp.zeros((n_sh,), jnp.int32), jnp.full((1,), n_sh, jnp.int32), layer,
                        sh_gate[:, None], sh_up[:, None], sh_down[:, None])
    return moe_combine(out_rows, eidx_flat, rank_flat, pad_starts, wts.reshape(-1), shared)


def kernel(x, c, ctx, c_ctx, ada_w, ada_b, w_in, conv_w, gdn_a_log, gdn_dt_bias, gdn_norm_w, q_norm_w, k_norm_w,
           w_out, ln1_w, ln1_b, router_w, router_bias, exp_w_gate, exp_w_up, exp_w_down, sh_w_gate, sh_w_up,
           sh_w_down, ln2_w, ln2_b):
    B, L, D = x.shape
    Lc = ctx.shape[1]
    Lt = Lc + L
    depth = ada_w.shape[0]
    alpha = (2.0 * depth) ** 0.25
    cos, sin = _rope_tables(Lc, L)
    xa = _standardize(jnp.concatenate([ctx, x], axis=1), LN_EPS)
    is_ctx = (jnp.arange(Lt) < Lc)[None, :, None]
    cond = jnp.concatenate([c, c_ctx[None], jnp.zeros((16 - B - 1, D), F32)], axis=0)
    cond = jax.nn.silu(cond)
    SH1, SC1, G1, SH2, SC2, G2 = range(6)
    mods = [(matmul(cond, ada_w[l], tm=16, tn=1024) + ada_b[l]).reshape(16, 1, 6 * D) for l in range(depth)]
    pick = lambda m, i: jnp.where(is_ctx, m[B, :, i * D:(i + 1) * D][None], m[:B, :, i * D:(i + 1) * D])
    h = (xa * (1.0 + pick(mods[0], SC1)) + pick(mods[0], SH1)).astype(BF16)
    for l in range(depth):
        last = l == depth - 1
        w_l = w_in[l]
        w_main = jnp.concatenate([w_l[:, :OFF_BA], w_l[:, OFF_ATT:]], axis=1).astype(BF16)
        w_ba = jnp.pad(w_l[:, OFF_BA:OFF_ATT], ((0, 0), (0, V7X_LANES - 4 * GDN_HEADS))).astype(BF16)
        y = _mixer(h.reshape(B * Lt, D), w_main, w_ba, conv_w[l], gdn_a_log[l], gdn_dt_bias[l], gdn_norm_w[l],
                   q_norm_w[l], k_norm_w[l], w_out[l].astype(BF16), cos, sin, B=B, Lt=Lt, n_ctx=Lc,
                   skip_ctx_queries=last)
        if last:
            xa = xa[:, Lc:]
        rows = xa.shape[1]
        n_ctx = 0 if last else Lc
        xa, h2 = residual_norm(xa, y, mods[l], G1, ln1_w[l], ln1_b[l], alpha=alpha, n_ctx=n_ctx, y_tiles=False,
                               h_mode="tiles", h_mod=mods[l], sc_chunk=SC2, sh_chunk=SH2)
        ff = _moe(h2.reshape(B * rows, V7X_SUBLANES, V7X_LANES), l, router_w[l].astype(BF16), router_bias[l],
                  exp_w_gate, exp_w_up, exp_w_down, sh_w_gate, sh_w_up, sh_w_down)
        ff = ff.reshape(B * rows * V7X_SUBLANES, V7X_LANES)
        if last:
            (xa,) = residual_norm(xa, ff, mods[l], G2, ln2_w[l], ln2_b[l], alpha=alpha, n_ctx=n_ctx, y_tiles=True)
        else:
            xa, h = residual_norm(xa, ff, mods[l], G2, ln2_w[l], ln2_b[l], alpha=alpha, n_ctx=n_ctx, y_tiles=True,
                                  h_mode="bf16", h_mod=mods[l + 1], sc_chunk=SC1, sh_chunk=SH1)
    return xa
```

```python
import functools

import jax
import jax.numpy as jnp
from jax import lax
from jax.experimental import pallas as pl
from jax.experimental.pallas import tpu as pltpu

F32 = jnp.float32
BF16 = jnp.bfloat16

GRID_W = 64
GDN_HEAD_DIM = 128
GDN_HEADS = 4
GDN_WIDTH = GDN_HEADS * GDN_HEAD_DIM
CONV_W = 5
CHUNK = 64
ATT_HEAD_DIM = 128
ATT_Q_HEADS = 4
ATT_KV_HEADS = 2
ATT_GROUP = ATT_Q_HEADS // ATT_KV_HEADS
ATT_WIDTH = ATT_Q_HEADS * ATT_HEAD_DIM
ATT_KV_WIDTH = ATT_KV_HEADS * ATT_HEAD_DIM
ROPE_AXIS_DIM = ATT_HEAD_DIM // 2
ROPE_THETA = 10000.0
OFF_Z = 3 * GDN_WIDTH
OFF_BA = 4 * GDN_WIDTH
OFF_ATT = OFF_BA + 4 * GDN_HEADS
N_EXPERTS = 256
TOP_K = 8
N_GROUPS = 8
TOPK_GROUPS = 4
EXPERT_DIM = 256
ROUTED_SCALE = 2.5
LN_EPS = 1e-5
RMS_EPS = 1e-6

V7X_LANES = 128
V7X_SUBLANES = 8
V7X_VMEM_BYTES = 64 * 1024 * 1024
VMEM_CAP = V7X_VMEM_BYTES - 8 * 1024 * 1024

EXPERT_ROWS = 256
GDN_CHUNKS_PER_ITER = 4
GDN_PREP_ROWS = 256
ROUTER_ROWS = 512
DISPATCH_TOKENS = 256
COMBINE_TOKENS = 128
DMA_TOKENS_PER_ITER = 2
RESNORM_ROWS = 256
ATT_Q_TILE = 128
ATT_KV_TILE = 256
ATT_PREP_ROWS = 256
MM_ROWS = 1024


def _vmem_limit(estimate_bytes):
    return int(min(VMEM_CAP, max(16 * 1024 * 1024, 2 * estimate_bytes)))


def _mm_body(a_ref, w_ref, o_ref):
    o_ref[...] = jnp.dot(a_ref[...].astype(BF16), w_ref[...].astype(BF16),
                         preferred_element_type=F32).astype(o_ref.dtype)


def matmul(a, w, *, tm, tn, out_dtype=F32):
    M, K = a.shape
    N = w.shape[1]
    assert M % tm == 0 and N % tn == 0, (M, N, tm, tn)
    est = 2 * (tm * K * a.dtype.itemsize + K * tn * w.dtype.itemsize + tm * tn * 4) + tm * K * 2 + K * tn * 2
    return pl.pallas_call(
        _mm_body,
        grid=(M // tm, N // tn),
        in_specs=[pl.BlockSpec((tm, K), lambda i, j: (i, 0)),
                  pl.BlockSpec((K, tn), lambda i, j: (0, j))],
        out_specs=pl.BlockSpec((tm, tn), lambda i, j: (i, j)),
        out_shape=jax.ShapeDtypeStruct((M, N), out_dtype),
        compiler_params=pltpu.CompilerParams(dimension_semantics=("parallel", "parallel"),
                                             vmem_limit_bytes=_vmem_limit(est)),
        name="proj_matmul",
    )(a, w)


def _unit_lower_inverses(a_list):
    n = a_list[0].shape[0]
    ii = lax.broadcasted_iota(jnp.int32, (n, n), 0)
    jj = lax.broadcasted_iota(jnp.int32, (n, n), 1)
    eye = jnp.where(ii == jj, 1.0, 0.0)
    mm = lambda a, b: jnp.dot(a.astype(BF16), b.astype(BF16), preferred_element_type=F32)
    pows = [-a for a in a_list]
    prods = [eye + p for p in pows]
    for _ in range(n.bit_length() - 2):
        pows = [mm(p, p) for p in pows]
        prods = [pr + mm(pr, p) for pr, p in zip(prods, pows)]
    return prods


def _gdn_body(pq_ref, pk_ref, pv_ref, pz_ref, cwq_ref, cwk_ref, cwv_ref, gcol_ref, grow_ref, nw_ref, o_ref,
              xpad_scr, q_ref, k_ref, v_ref, u_scr, wq_scr, kd_scr, qk_scr, cd_scr,
              *, n_chunks, n_ctx_chunks, chunks_per_iter):
    C = CHUNK
    Lt = n_chunks * C
    n_ctx = n_ctx_chunks * C
    RB = GDN_PREP_ROWS
    PAD = V7X_SUBLANES

    def prep(src_ref, w_ref, dst_ref, unit_rows, scale):
        xpad_scr[0:PAD, :] = jnp.zeros((PAD, GDN_HEAD_DIM), F32)
        xpad_scr[PAD + Lt:2 * PAD + Lt, :] = jnp.zeros((PAD, GDN_HEAD_DIM), F32)
        xpad_scr[PAD:PAD + Lt, :] = src_ref[0]
        for blk in range(Lt // RB):
            r0 = blk * RB
            t = r0 + lax.broadcasted_iota(jnp.int32, (RB, 1), 0)
            seg_lo = jnp.where(t < n_ctx, 0, n_ctx)
            seg_hi = jnp.where(t < n_ctx, n_ctx, Lt)
            acc = jnp.zeros((RB, GDN_HEAD_DIM), F32)
            for i in range(CONV_W):
                off = i - CONV_W // 2
                xs = xpad_scr[PAD + r0 + off:PAD + r0 + off + RB, :]
                if off != 0:
                    xs = jnp.where(jnp.logical_and(t + off >= seg_lo, t + off < seg_hi), xs, 0.0)
                acc = acc + xs * w_ref[0, i:i + 1, :]
            y = acc * jax.nn.sigmoid(acc)
            if unit_rows:
                y = y * (lax.rsqrt(jnp.sum(y * y, axis=1, keepdims=True) + RMS_EPS) * scale)
            dst_ref[r0:r0 + RB, :] = y

    prep(pq_ref, cwq_ref, q_ref, True, GDN_HEAD_DIM ** -0.5)
    prep(pk_ref, cwk_ref, k_ref, True, 1.0)
    prep(pv_ref, cwv_ref, v_ref, False, 1.0)

    dn_t = (((1,), (1,)), ((), ()))
    ii = lax.broadcasted_iota(jnp.int32, (C, C), 0)
    jj = lax.broadcasted_iota(jnp.int32, (C, C), 1)
    incl = (ii >= jj, ii <= jj)
    strict = (ii > jj, ii < jj)

    def pre_iter(it, carry):
        chunks = [it * chunks_per_iter + s for s in range(chunks_per_iter)]
        rows = [pl.multiple_of(c * C, C) for c in chunks]
        qs = [q_ref[pl.ds(r0, C), :] for r0 in rows]
        ks = [k_ref[pl.ds(r0, C), :] for r0 in rows]
        vs = [v_ref[pl.ds(r0, C), :] for r0 in rows]
        kbs = [k.astype(BF16) for k in ks]
        kks = [lax.dot_general(kb, kb, dn_t, preferred_element_type=F32) for kb in kbs]
        qk_raws = [lax.dot_general(q.astype(BF16), kb, dn_t, preferred_element_type=F32) for q, kb in zip(qs, kbs)]
        combos = [(s, d) for s in range(chunks_per_iter) for d in (0, 1)]
        terms = []
        for s, d in combos:
            r0, c = rows[s], chunks[s]
            g_c = gcol_ref[0, 0, pl.ds(r0, C), d:d + 1]
            b_c = gcol_ref[0, 0, pl.ds(r0, C), 2 + d:3 + d]
            g_r = grow_ref[0, 0, d, pl.ds(c, 1), :]
            cum_col = jnp.sum(jnp.where(incl[d], g_r, 0.0), axis=1, keepdims=True)
            cum_row = jnp.sum(jnp.where(incl[1 - d], g_c, 0.0), axis=0, keepdims=True)
            cum_last = jnp.sum(g_r, axis=1, keepdims=True)
            decay = jnp.where(incl[d], jnp.exp(jnp.where(incl[d], cum_col - cum_row, 0.0)), 0.0)
            a_mat = jnp.where(strict[d], b_c * kks[s] * decay, 0.0)
            terms.append((b_c, cum_col, cum_last, decay, a_mat))
        t_invs = _unit_lower_inverses([t[4] for t in terms])
        sols = []
        for (s, d), (b_c, cum_col, cum_last, decay, _), t_inv in zip(combos, terms, t_invs):
            e_g = jnp.exp(cum_col)
            rhs = jnp.concatenate([vs[s] * b_c, ks[s] * (b_c * e_g)], axis=1)
            sols.append((e_g, jnp.dot(t_inv.astype(BF16), rhs.astype(BF16), preferred_element_type=F32)))
        for (s, d), (b_c, cum_col, cum_last, decay, _), (e_g, sol) in zip(combos, terms, sols):
            r0, c = rows[s], chunks[s]
            r1 = pl.multiple_of(c * 2 * C, 2 * C)
            u_scr[d, pl.ds(r0, C), :] = sol[:, :GDN_HEAD_DIM]
            wq_scr[d, pl.ds(r1, C), :] = sol[:, GDN_HEAD_DIM:].astype(BF16)
            wq_scr[d, pl.ds(r1 + C, C), :] = (qs[s] * e_g).astype(BF16)
            kd_scr[d, pl.ds(r0, C), :] = (ks[s] * jnp.exp(cum_last - cum_col)).astype(BF16)
            qk_scr[d, pl.ds(r0, C), :] = (qk_raws[s] * decay).astype(BF16)
            cd_scr[d, pl.ds(c, 1), :] = jnp.broadcast_to(jnp.exp(cum_last), (1, GDN_HEAD_DIM))
        return carry

    lax.fori_loop(0, n_chunks // chunks_per_iter, pre_iter, 0)

    o_ref[...] = jnp.zeros_like(o_ref)

    def step(t, states):
        c_bwd = jnp.where(t < n_ctx_chunks, n_ctx_chunks - 1 - t, n_chunks - 1 + n_ctx_chunks - t)
        chunks = (t, c_bwd)
        rows = [pl.multiple_of(c * C, C) for c in chunks]
        ws_qs = [jnp.dot(wq_scr[d, pl.ds(pl.multiple_of(chunks[d] * 2 * C, 2 * C), 2 * C), :],
                         states[d].astype(BF16), preferred_element_type=F32) for d in (0, 1)]
        v_new_b = [(u_scr[d, pl.ds(rows[d], C), :] - ws_qs[d][:C]).astype(BF16) for d in (0, 1)]
        intra = [jnp.dot(qk_scr[d, pl.ds(rows[d], C), :], v_new_b[d], preferred_element_type=F32) for d in (0, 1)]
        upd = [lax.dot_general(kd_scr[d, pl.ds(rows[d], C), :], v_new_b[d], (((0,), (0,)), ((), ())),
                               preferred_element_type=F32) for d in (0, 1)]
        for d in (0, 1):
            o_ref[0, pl.ds(rows[d], C), :] += ws_qs[d][C:] + intra[d]
        return tuple(states[d] * cd_scr[d, pl.ds(chunks[d], 1), :] + upd[d] for d in (0, 1))

    s0 = jnp.zeros((GDN_HEAD_DIM, GDN_HEAD_DIM), F32)
    lax.fori_loop(0, n_chunks, step, (s0, s0))

    for blk in range(Lt // RB):
        r0 = blk * RB
        o = o_ref[0, r0:r0 + RB, :]
        z = pz_ref[0, r0:r0 + RB, :]
        y = o * lax.rsqrt(jnp.mean(o * o, axis=1, keepdims=True) + RMS_EPS) * nw_ref[...]
        o_ref[0, r0:r0 + RB, :] = y * (z * jax.nn.sigmoid(z))


def gated_delta_heads(p, conv_w, gcol, grow, norm_w, *, n_ctx_chunks):
    B, Lt = p.shape[:2]
    H, dk = GDN_HEADS, GDN_HEAD_DIM
    n_chunks = Lt // CHUNK
    chunks_per_iter = GDN_CHUNKS_PER_ITER
    assert n_chunks % chunks_per_iter == 0 and Lt % GDN_PREP_ROWS == 0
    cd_rows = -(-n_chunks // 8) * 8
    col_spec = lambda first: pl.BlockSpec((1, Lt, dk), lambda b, h: (b, 0, first + h))
    w_spec = lambda first: pl.BlockSpec((1, CONV_W, dk), lambda b, h: (first + h, 0, 0))
    scratch = [pltpu.VMEM((Lt + 2 * V7X_SUBLANES, dk), F32),
               pltpu.VMEM((Lt, dk), F32),
               pltpu.VMEM((Lt, dk), F32),
               pltpu.VMEM((Lt, dk), F32),
               pltpu.VMEM((2, Lt, dk), F32),
               pltpu.VMEM((2, 2 * Lt, dk), BF16),
               pltpu.VMEM((2, Lt, dk), BF16),
               pltpu.VMEM((2, Lt, CHUNK), BF16),
               pltpu.VMEM((2, cd_rows, dk), F32)]
    est = (2 * 5 * Lt * dk * 4
           + 2 * Lt * V7X_LANES * 4
           + 4 * Lt * dk * 4
           + 2 * Lt * dk * 4 + 2 * 2 * Lt * dk * 2 + 2 * Lt * dk * 2 + 2 * Lt * V7X_LANES * 2)
    body = functools.partial(_gdn_body, n_chunks=n_chunks, n_ctx_chunks=n_ctx_chunks,
                             chunks_per_iter=chunks_per_iter)
    return pl.pallas_call(
        body,
        grid=(B, H),
        in_specs=[col_spec(0), col_spec(H), col_spec(2 * H), col_spec(3 * H),
                  w_spec(0), w_spec(H), w_spec(2 * H),
                  pl.BlockSpec((1, 1, Lt, 4), lambda b, h: (b, h, 0, 0)),
                  pl.BlockSpec((1, 1, 2, n_chunks, CHUNK), lambda b, h: (b, h, 0, 0, 0)),
                  pl.BlockSpec((1, dk), lambda b, h: (0, 0))],
        out_specs=pl.BlockSpec((1, Lt, dk), lambda b, h: (b, 0, h)),
        out_shape=jax.ShapeDtypeStruct((B, Lt, H * dk), F32),
        scratch_shapes=scratch,
        compiler_params=pltpu.CompilerParams(dimension_semantics=("parallel", "parallel"),
                                             vmem_limit_bytes=int(min(VMEM_CAP, est + est // 8))),
        name="gated_delta_scan",
    )(p, p, p, p, conv_w, conv_w, conv_w, gcol, grow, norm_w)


def _attn_prep_body(a_ref, cos_ref, sin_ref, qw_ref, kw_ref, q_ref, kt_ref, v_ref):
    hd = ATT_HEAD_DIM
    tp = a_ref.shape[1]
    cos = cos_ref[...]
    sin = sin_ref[...]
    lane = lax.broadcasted_iota(jnp.int32, (tp, hd), 1)
    first_half = (lane // (ROPE_AXIS_DIM // 2)) % 2 == 0

    def norm_rope(x, w):
        x = x * lax.rsqrt(jnp.mean(x * x, axis=1, keepdims=True) + RMS_EPS) * w
        partner = jnp.where(first_half, pltpu.roll(x, hd - ROPE_AXIS_DIM // 2, axis=1),
                            pltpu.roll(x, ROPE_AXIS_DIM // 2, axis=1))
        return x * cos + partner * sin

    for h in range(ATT_Q_HEADS):
        q = norm_rope(a_ref[0, :, h * hd:(h + 1) * hd], qw_ref[...])
        q_ref[0, h] = (q * hd ** -0.5).astype(BF16)
    for h in range(ATT_KV_HEADS):
        k = norm_rope(a_ref[0, :, ATT_WIDTH + h * hd:ATT_WIDTH + (h + 1) * hd], kw_ref[...])
        kt_ref[0, h] = k.T.astype(BF16)
        c0 = ATT_WIDTH + ATT_KV_WIDTH + h * hd
        v_ref[0, h] = a_ref[0, :, c0:c0 + hd].astype(BF16)


def attention_inputs(p, cos, sin, q_norm_w, k_norm_w):
    B, Lt, n_cols = p.shape
    hd = ATT_HEAD_DIM
    width = ATT_WIDTH + 2 * ATT_KV_WIDTH
    assert n_cols % width == 0 and Lt % ATT_PREP_ROWS == 0
    tp = ATT_PREP_ROWS
    est = 2 * (tp * width * 4 + 2 * tp * hd * 4 + tp * width * 2) + 8 * tp * hd * 4
    return pl.pallas_call(
        _attn_prep_body,
        grid=(B, Lt // tp),
        in_specs=[pl.BlockSpec((1, tp, width), lambda b, i: (b, i, n_cols // width - 1)),
                  pl.BlockSpec((tp, hd), lambda b, i: (i, 0)),
                  pl.BlockSpec((tp, hd), lambda b, i: (i, 0)),
                  pl.BlockSpec((1, hd), lambda b, i: (0, 0)),
                  pl.BlockSpec((1, hd), lambda b, i: (0, 0))],
        out_specs=[pl.BlockSpec((1, ATT_Q_HEADS, tp, hd), lambda b, i: (b, 0, i, 0)),
                   pl.BlockSpec((1, ATT_KV_HEADS, hd, tp), lambda b, i: (b, 0, 0, i)),
                   pl.BlockSpec((1, ATT_KV_HEADS, tp, hd), lambda b, i: (b, 0, i, 0))],
        out_shape=[jax.ShapeDtypeStruct((B, ATT_Q_HEADS, Lt, hd), BF16),
                   jax.ShapeDtypeStruct((B, ATT_KV_HEADS, hd, Lt), BF16),
                   jax.ShapeDtypeStruct((B, ATT_KV_HEADS, Lt, hd), BF16)],
        compiler_params=pltpu.CompilerParams(dimension_semantics=("parallel", "parallel"),
                                             vmem_limit_bytes=_vmem_limit(est)),
        name="attention_inputs",
    )(p, cos, sin, q_norm_w.reshape(1, hd), k_norm_w.reshape(1, hd))


def _attn_body(q_ref, kt_ref, v_ref, o_ref, s_scr, *, n_ctx, first_tile):
    G, tq, hd = q_ref.shape[2:]
    rows = G * tq
    tk = s_scr.shape[2]
    tile = pl.program_id(2) + first_tile
    q = q_ref[0, 0].reshape(rows, hd)

    def run(n_blocks):
        m = None
        for j in range(n_blocks):
            s = jnp.dot(q, kt_ref[0, 0, :, j * tk:(j + 1) * tk], preferred_element_type=F32)
            s_scr[j] = s
            bm = jnp.max(s, axis=1, keepdims=True)
            m = bm if m is None else jnp.maximum(m, bm)
        denom = jnp.zeros((rows, 1), F32)
        acc = jnp.zeros((rows, hd), F32)
        for j in range(n_blocks):
            p = jnp.exp(s_scr[j] - m)
            denom = denom + jnp.sum(p, axis=1, keepdims=True)
            acc = acc + jnp.dot(p.astype(BF16), v_ref[0, 0, j * tk:(j + 1) * tk, :], preferred_element_type=F32)
        o = acc / denom
        for g in range(G):
            o_ref[0, :, g * hd:(g + 1) * hd] = o[g * tq:(g + 1) * tq]

    if first_tile * tq < n_ctx:
        pl.when(tile * tq < n_ctx)(lambda: run(n_ctx // tk))
        pl.when(tile * tq >= n_ctx)(lambda: run(v_ref.shape[2] // tk))
    else:
        run(v_ref.shape[2] // tk)


def attention(q, kt, v, *, n_ctx, skip_ctx_queries):
    B, Hkv, G, Lt, hd = q.shape
    tq = ATT_Q_TILE
    assert Lt % tq == 0 and n_ctx % tq == 0
    first_tile = n_ctx // tq if skip_ctx_queries else 0
    n_tiles = Lt // tq - first_tile
    tk = ATT_KV_TILE
    assert Lt % tk == 0 and n_ctx % tk == 0
    rows = G * tq
    scratch = [pltpu.VMEM((Lt // tk, rows, tk), F32)]
    est = 2 * (rows * hd * 2 + 2 * Lt * hd * 2 + tq * G * hd * 4) + rows * Lt * 4 + 8 * rows * tk * 4
    body = functools.partial(_attn_body, n_ctx=n_ctx, first_tile=first_tile)
    return pl.pallas_call(
        body,
        grid=(B, Hkv, n_tiles),
        in_specs=[pl.BlockSpec((1, 1, G, tq, hd), lambda b, h, i: (b, h, 0, i + first_tile, 0)),
                  pl.BlockSpec((1, 1, hd, Lt), lambda b, h, i: (b, h, 0, 0)),
                  pl.BlockSpec((1, 1, Lt, hd), lambda b, h, i: (b, h, 0, 0))],
        out_specs=pl.BlockSpec((1, tq, G * hd), lambda b, h, i: (b, i, h)),
        out_shape=jax.ShapeDtypeStruct((B, n_tiles * tq, Hkv * G * hd), F32),
        scratch_shapes=scratch,
        compiler_params=pltpu.CompilerParams(dimension_semantics=("parallel", "parallel", "parallel"),
                                             vmem_limit_bytes=_vmem_limit(est // 2)),
        name="gqa_attention",
    )(q, kt, v)


def _rows_2d(x_ref):
    tm = x_ref.shape[0] // V7X_SUBLANES
    return jnp.concatenate([x_ref[pl.ds(c, tm, stride=V7X_SUBLANES), :] for c in range(V7X_SUBLANES)], axis=1)


def _router_body(h_ref, w_ref, bias_ref, eidx_ref, wts_ref, rank_ref, counts_ref, carry_scr):
    tm = h_ref.shape[0] // V7X_SUBLANES
    E = N_EXPERTS
    per_group = E // N_GROUPS
    neg = -jnp.inf

    @pl.when(pl.program_id(0) == 0)
    def _():
        carry_scr[...] = jnp.zeros_like(carry_scr)

    logits = jnp.dot(_rows_2d(h_ref).astype(BF16), w_ref[...], preferred_element_type=F32)
    s = jax.nn.sigmoid(logits)
    sel = s + bias_ref[...]
    lane = lax.broadcasted_iota(jnp.int32, (tm, E), 1)
    lane_f = lane.astype(F32)
    grp = lane // per_group
    scores = []
    for g in range(N_GROUPS):
        v = jnp.where(grp == g, sel, neg)
        m1 = jnp.max(v, axis=1, keepdims=True)
        i1 = jnp.min(jnp.where(v == m1, lane_f, float(E)), axis=1, keepdims=True)
        m2 = jnp.max(jnp.where(lane_f == i1, neg, v), axis=1, keepdims=True)
        scores.append(m1 + m2)
    allowed = jnp.zeros((tm, E), jnp.bool_)
    for g in range(N_GROUPS):
        beaten = jnp.zeros((tm, 1), F32)
        for o in range(N_GROUPS):
            if o != g:
                wins = (scores[o] > scores[g]) if o > g else (scores[o] >= scores[g])
                beaten = beaten + wins.astype(F32)
        allowed = jnp.logical_or(allowed, jnp.logical_and(grp == g, beaten < TOPK_GROUPS))
    masked = jnp.where(allowed, sel, neg)
    idx_cols, s_cols = [], []
    chosen = jnp.zeros((tm, E), jnp.bool_)
    for _ in range(TOP_K):
        m = jnp.max(masked, axis=1, keepdims=True)
        idx = jnp.min(jnp.where(masked == m, lane_f, float(E)), axis=1, keepdims=True)
        hit = lane_f == idx
        s_cols.append(jnp.sum(jnp.where(hit, s, 0.0), axis=1, keepdims=True))
        idx_cols.append(idx)
        chosen = jnp.logical_or(chosen, hit)
        masked = jnp.where(hit, neg, masked)
    ri = lax.broadcasted_iota(jnp.int32, (tm, tm), 0)
    ci = lax.broadcasted_iota(jnp.int32, (tm, tm), 1)
    running = jnp.dot((ci < ri).astype(BF16), chosen.astype(BF16), preferred_element_type=F32) + carry_scr[...]
    total = s_cols[0]
    for col in s_cols[1:]:
        total = total + col
    k_lane = lax.broadcasted_iota(jnp.int32, (tm, TOP_K), 1)
    eidx = jnp.zeros((tm, TOP_K), F32)
    wts = jnp.zeros((tm, TOP_K), F32)
    rank = jnp.zeros((tm, TOP_K), F32)
    for j in range(TOP_K):
        r_j = jnp.sum(jnp.where(lane_f == idx_cols[j], running, 0.0), axis=1, keepdims=True)
        eidx = jnp.where(k_lane == j, idx_cols[j], eidx)
        wts = jnp.where(k_lane == j, s_cols[j] / total * ROUTED_SCALE, wts)
        rank = jnp.where(k_lane == j, r_j, rank)
    eidx_ref[...] = eidx.astype(jnp.int32)
    wts_ref[...] = wts
    rank_ref[...] = rank.astype(jnp.int32)
    carry_scr[...] = carry_scr[...] + jnp.sum(chosen.astype(F32), axis=0, keepdims=True)
    counts_ref[...] = carry_scr[...].astype(jnp.int32)


def moe_router(h, router_w_bf, router_bias):
    T = h.shape[0]
    D, E = router_w_bf.shape
    tm = ROUTER_ROWS
    assert T % tm == 0
    tok = pl.BlockSpec((tm, TOP_K), lambda i: (i, 0))
    est = 2 * (tm * D * 4 + D * E * 2) + tm * tm * 4 + 24 * tm * E * 4
    h = h.reshape(T * V7X_SUBLANES, V7X_LANES)
    return pl.pallas_call(
        _router_body,
        grid=(T // tm,),
        in_specs=[pl.BlockSpec((tm * V7X_SUBLANES, V7X_LANES), lambda i: (i, 0)),
                  pl.BlockSpec((D, E), lambda i: (0, 0)),
                  pl.BlockSpec((1, E), lambda i: (0, 0))],
        out_specs=[tok, tok, tok, pl.BlockSpec((1, E), lambda i: (0, 0))],
        out_shape=[jax.ShapeDtypeStruct((T, TOP_K), jnp.int32), jax.ShapeDtypeStruct((T, TOP_K), F32),
                   jax.ShapeDtypeStruct((T, TOP_K), jnp.int32), jax.ShapeDtypeStruct((1, E), jnp.int32)],
        scratch_shapes=[pltpu.VMEM((1, E), F32)],
        compiler_params=pltpu.CompilerParams(dimension_semantics=("arbitrary",),
                                             vmem_limit_bytes=_vmem_limit(est // 2)),
        name="moe_router",
    )(h, router_w_bf, router_bias.reshape(1, E))


def _dispatch_body(pstart_ref, pad_lo_ref, pad_hi_ref, eidx_ref, rank_ref, h_ref, xs_hbm, dest_ref, zero_scr, sems):
    i = pl.program_id(0)
    tt = h_ref.shape[0]

    def row_copy(src, dst_row, sem):
        return pltpu.make_async_copy(src, xs_hbm.at[dst_row], sem)

    @pl.when(i == 0)
    def _():
        zero_scr[...] = jnp.zeros_like(zero_scr)
        for wait in (False, True):
            def per_expert(e, carry, wait=wait):
                def per_row(r, c):
                    cp = row_copy(zero_scr, r, sems.at[1])
                    cp.wait() if wait else cp.start()
                    return c
                return lax.fori_loop(pad_lo_ref[e], pad_hi_ref[e], per_row, carry)
            lax.fori_loop(0, N_EXPERTS, per_expert, 0)

    def issue(p, c):
        toks = [p * DMA_TOKENS_PER_ITER + u for u in range(DMA_TOKENS_PER_ITER)]
        dst = [[pstart_ref[eidx_ref[r * TOP_K + j]] + rank_ref[r * TOP_K + j] for j in range(TOP_K)]
               for r in toks]
        for r, dst_r in zip(toks, dst):
            for j, d in enumerate(dst_r):
                dest_ref[r * TOP_K + j] = d
        for r, dst_r in zip(toks, dst):
            for j, d in enumerate(dst_r):
                row_copy(h_ref.at[r], d, sems.at[0]).start(priority=j % 2)
        return c
    lax.fori_loop(0, tt // DMA_TOKENS_PER_ITER, issue, 0)

    def drain(r, c):
        for j in range(TOP_K):
            row_copy(h_ref.at[0], 0, sems.at[0]).wait()
        return c
    lax.fori_loop(0, tt, drain, 0)


def moe_dispatch(h, eidx_flat, rank_flat, pad_starts, pad_lo, pad_hi, n_rows):
    T = h.shape[0]
    tile = h.shape[1:]
    tt = DISPATCH_TOKENS
    assert T % tt == 0
    idx_spec = pl.BlockSpec((tt * TOP_K,), lambda i, *_: (i,), memory_space=pltpu.SMEM)
    grid_spec = pltpu.PrefetchScalarGridSpec(
        num_scalar_prefetch=3, grid=(T // tt,),
        in_specs=[idx_spec, idx_spec, pl.BlockSpec((tt,) + tile, lambda i, *_: (i, 0, 0))],
        out_specs=[pl.BlockSpec(memory_space=pl.ANY), idx_spec],
        scratch_shapes=[pltpu.VMEM(tile, h.dtype), pltpu.SemaphoreType.DMA((2,))])
    return pl.pallas_call(
        _dispatch_body,
        grid_spec=grid_spec,
        out_shape=[jax.ShapeDtypeStruct((n_rows,) + tile, h.dtype),
                   jax.ShapeDtypeStruct((T * TOP_K,), jnp.int32)],
        compiler_params=pltpu.CompilerParams(dimension_semantics=("arbitrary",)),
        name="moe_dispatch",
    )(pad_starts, pad_lo, pad_hi, eidx_flat, rank_flat, h)


def _combine_body(dest_ref, wts_ref, shared_ref, rows_hbm, o_ref, buf, sems, *, tt, n_tiles):
    i = pl.program_id(0)

    def row_copy(src_row, slot, j, r):
        return pltpu.make_async_copy(rows_hbm.at[src_row], buf.at[slot, j, r], sems.at[slot])

    @pl.when(i < n_tiles)
    def _():
        slot = i % 2

        def issue(p, c):
            toks = [p * DMA_TOKENS_PER_ITER + u for u in range(DMA_TOKENS_PER_ITER)]
            src = [[dest_ref[r * TOP_K + j] for j in range(TOP_K)] for r in toks]
            for r, src_r in zip(toks, src):
                for j, s in enumerate(src_r):
                    row_copy(s, slot, j, r).start(priority=j % 2)
            return c
        lax.fori_loop(0, tt // DMA_TOKENS_PER_ITER, issue, 0)

    @pl.when(i >= 1)
    def _():
        slot = (i + 1) % 2

        def drain(r, c):
            for j in range(TOP_K):
                row_copy(0, slot, j, 0).wait()
            return c
        lax.fori_loop(0, tt, drain, 0)

        def per_token(r, c):
            acc = shared_ref[r]
            for j in range(TOP_K):
                acc = acc + buf[slot, j, r] * wts_ref[r * TOP_K + j]
            o_ref[r] = acc
            return c
        lax.fori_loop(0, tt, per_token, 0, unroll=4)


def moe_combine(rows, dest_flat, wts_flat, shared):
    T = shared.shape[0]
    tile = shared.shape[1:]
    tt = COMBINE_TOKENS
    assert T % tt == 0
    n_tiles = T // tt
    prev = lambda i: (jnp.maximum(i - 1, 0), 0, 0)
    idx_spec = pl.BlockSpec((tt * TOP_K,), lambda i: (jnp.minimum(i, n_tiles - 1),), memory_space=pltpu.SMEM)
    wts_spec = pl.BlockSpec((tt * TOP_K,), lambda i: (jnp.maximum(i - 1, 0),), memory_space=pltpu.SMEM)
    grid_spec = pltpu.PrefetchScalarGridSpec(
        num_scalar_prefetch=0, grid=(n_tiles + 1,),
        in_specs=[idx_spec, wts_spec, pl.BlockSpec((tt,) + tile, prev),
                  pl.BlockSpec(memory_space=pl.ANY)],
        out_specs=pl.BlockSpec((tt,) + tile, prev),
        scratch_shapes=[pltpu.VMEM((2, TOP_K, tt) + tile, F32), pltpu.SemaphoreType.DMA((2,))])
    row_bytes = tile[0] * tile[1] * 4
    est = 2 * TOP_K * tt * row_bytes + 4 * tt * row_bytes
    return pl.pallas_call(
        functools.partial(_combine_body, tt=tt, n_tiles=n_tiles),
        grid_spec=grid_spec,
        out_shape=jax.ShapeDtypeStruct((T,) + tile, F32),
        compiler_params=pltpu.CompilerParams(dimension_semantics=("arbitrary",),
                                             vmem_limit_bytes=_vmem_limit(est)),
        name="moe_combine",
    )(dest_flat, wts_flat, shared, rows)


def _expert_body(blk_e_ref, n_used_ref, x_ref, wg_ref, wu_ref, wd_ref, o_ref, wg_b, wu_b, wd_b):
    i = pl.program_id(0)

    @pl.when(jnp.logical_or(i == 0, blk_e_ref[i] != blk_e_ref[jnp.maximum(i - 1, 0)]))
    def _():
        wg_b[...] = wg_ref[0, 0].astype(BF16)
        wu_b[...] = wu_ref[0, 0].astype(BF16)
        wd_b[...] = wd_ref[0, 0].astype(BF16)

    @pl.when(i < n_used_ref[0])
    def _():
        x = _rows_2d(x_ref).astype(BF16)
        a = jnp.dot(x, wg_b[...], preferred_element_type=F32)
        b = jnp.dot(x, wu_b[...], preferred_element_type=F32)
        h = (a * jax.nn.sigmoid(a)) * b
        o = jnp.dot(h.astype(BF16), wd_b[...], preferred_element_type=F32)
        tm = o.shape[0]
        for c in range(V7X_SUBLANES):
            o_ref[pl.ds(c, tm, stride=V7X_SUBLANES), :] = o[:, c * V7X_LANES:(c + 1) * V7X_LANES]


def expert_mlp(xs, blk_e, n_used, layer, w_gate, w_up, w_down):
    P = xs.shape[0]
    tile = xs.shape[1:]
    D = tile[0] * tile[1]
    F = w_gate.shape[-1]
    tm = EXPERT_ROWS
    n_blocks = P // tm
    xs = xs.reshape(P * tile[0], tile[1])

    def row_map(i, blk_e_ref, n_used_ref):
        return (jnp.minimum(i, n_used_ref[0] - 1), 0)

    def w_map(i, blk_e_ref, n_used_ref):
        return (layer, blk_e_ref[i], 0, 0)

    est = 2 * (2 * tm * D * 4 + 3 * D * F * 4) + 3 * D * F * 2 + tm * D * 2 + 3 * tm * F * 4
    grid_spec = pltpu.PrefetchScalarGridSpec(
        num_scalar_prefetch=2,
        grid=(n_blocks,),
        in_specs=[pl.BlockSpec((tm * tile[0], tile[1]), row_map),
                  pl.BlockSpec((1, 1, D, F), w_map),
                  pl.BlockSpec((1, 1, D, F), w_map),
                  pl.BlockSpec((1, 1, F, D), w_map)],
        out_specs=pl.BlockSpec((tm * tile[0], tile[1]), row_map),
        scratch_shapes=[pltpu.VMEM((D, F), BF16), pltpu.VMEM((D, F), BF16), pltpu.VMEM((F, D), BF16)],
    )
    out = pl.pallas_call(
        _expert_body,
        grid_spec=grid_spec,
        out_shape=jax.ShapeDtypeStruct(xs.shape, F32),
        compiler_params=pltpu.CompilerParams(dimension_semantics=("arbitrary",),
                                             vmem_limit_bytes=_vmem_limit(est)),
        name="expert_mlp",
    )(blk_e, n_used, xs, w_gate, w_up, w_down)
    return out.reshape((P,) + tile)


def _resnorm_body(*refs, alpha, y_tiles, h_mode):
    x_ref, y_ref, g_ref = refs[:3]
    if h_mode is None:
        w_ref, b_ref, xo_ref = refs[3:]
    else:
        sc_ref, sh_ref, w_ref, b_ref, xo_ref, ho_ref = refs[3:]
    y = _rows_2d(y_ref) if y_tiles else y_ref[0]
    v = alpha * x_ref[0] + g_ref[0] * y
    mu = jnp.mean(v, axis=1, keepdims=True)
    d = v - mu
    xn = d * lax.rsqrt(jnp.mean(d * d, axis=1, keepdims=True) + LN_EPS) * w_ref[...] + b_ref[...]
    xo_ref[0] = xn
    if h_mode is not None:
        h = xn * (1.0 + sc_ref[0]) + sh_ref[0]
        if h_mode == "bf16":
            ho_ref[0] = h.astype(BF16)
        else:
            rows = h.shape[0]
            for c in range(V7X_SUBLANES):
                ho_ref[pl.ds(c, rows, stride=V7X_SUBLANES), :] = h[:, c * V7X_LANES:(c + 1) * V7X_LANES]


def residual_norm(x, y, g_mod, g_chunk, ln_w, ln_b, *, alpha, n_ctx, y_tiles, h_mode=None, h_mod=None,
                  sc_chunk=None, sh_chunk=None):
    B, R, D = x.shape
    tr = RESNORM_ROWS
    assert R % tr == 0 and n_ctx % tr == 0
    n_ctx_tiles = n_ctx // tr
    n_i = R // tr

    def mod_spec(chunk):
        return pl.BlockSpec((1, 1, D), lambda b, i: (jnp.where(i < n_ctx_tiles, B, b), 0, chunk))

    row_spec = pl.BlockSpec((1, tr, D), lambda b, i: (b, i, 0))
    tile_spec = pl.BlockSpec((tr * V7X_SUBLANES, V7X_LANES), lambda b, i: (b * n_i + i, 0))
    vec_spec = pl.BlockSpec((1, D), lambda b, i: (0, 0))
    in_specs = [row_spec, tile_spec if y_tiles else row_spec, mod_spec(g_chunk)]
    args = [x, y if y_tiles else y.reshape(B, R, D), g_mod]
    out_shape = [jax.ShapeDtypeStruct((B, R, D), F32)]
    out_specs = [row_spec]
    if h_mode is not None:
        in_specs += [mod_spec(sc_chunk), mod_spec(sh_chunk)]
        args += [h_mod, h_mod]
        if h_mode == "bf16":
            out_shape.append(jax.ShapeDtypeStruct((B, R, D), BF16))
            out_specs.append(row_spec)
        else:
            out_shape.append(jax.ShapeDtypeStruct((B * R * V7X_SUBLANES, V7X_LANES), F32))
            out_specs.append(tile_spec)
    in_specs += [vec_spec, vec_spec]
    args += [ln_w.reshape(1, D), ln_b.reshape(1, D)]
    est = 2 * 4 * tr * D * 4 + 6 * tr * D * 4
    return pl.pallas_call(
        functools.partial(_resnorm_body, alpha=alpha, y_tiles=y_tiles, h_mode=h_mode),
        grid=(B, n_i),
        in_specs=in_specs,
        out_specs=out_specs,
        out_shape=out_shape,
        compiler_params=pltpu.CompilerParams(dimension_semantics=("parallel", "parallel"),
                                             vmem_limit_bytes=_vmem_limit(est)),
        name="residual_norm",
    )(*args)


def _standardize(x, eps):
    mu = jnp.mean(x, -1, keepdims=True)
    var = jnp.mean(jnp.square(x - mu), -1, keepdims=True)
    return (x - mu) * lax.rsqrt(var + eps)


def _rope_tables(n_ctx, n_lat):
    rows = n_lat // GRID_W
    row = jnp.repeat(jnp.arange(rows, dtype=F32), GRID_W)
    col = jnp.tile(jnp.arange(GRID_W, dtype=F32), rows)
    inv_freq = ROPE_THETA ** (-jnp.arange(0, ROPE_AXIS_DIM, 2, dtype=F32) / ROPE_AXIS_DIM)
    ang = jnp.stack([row[:, None] * inv_freq, col[:, None] * inv_freq], axis=1)
    cos = jnp.concatenate([jnp.ones((n_ctx,) + ang.shape[1:], F32), jnp.cos(ang)], axis=0)
    sin = jnp.concatenate([jnp.zeros((n_ctx,) + ang.shape[1:], F32), jnp.sin(ang)], axis=0)
    cos = jnp.stack([cos, cos], axis=2).reshape(n_ctx + n_lat, ATT_HEAD_DIM)
    sin = jnp.stack([-sin, sin], axis=2).reshape(n_ctx + n_lat, ATT_HEAD_DIM)
    return cos, sin


def _mixer(h_bf, w_main, w_ba, conv_w, a_log, dt_bias, gdn_norm_w, q_norm_w, k_norm_w, w_out_bf, cos, sin,
           *, B, Lt, n_ctx, skip_ctx_queries):
    T = B * Lt
    p = matmul(h_bf, w_main, tm=MM_ROWS, tn=1024)
    p_ba = matmul(h_bf, w_ba, tm=MM_ROWS, tn=V7X_LANES)
    p = p.reshape(B, Lt, -1)
    ba = p_ba[:, :4 * GDN_HEADS].reshape(B, Lt, 2, 2, GDN_HEADS)
    beta = jax.nn.sigmoid(ba[:, :, 0])
    g = -jnp.exp(a_log) * jax.nn.softplus(ba[:, :, 1] + dt_bias)
    gcol = jnp.concatenate([g, beta], axis=2).transpose(0, 3, 1, 2)
    grow = g.transpose(0, 3, 2, 1).reshape(B, GDN_HEADS, 2, Lt // CHUNK, CHUNK)
    conv_heads = conv_w.reshape(CONV_W, 3 * GDN_HEADS, GDN_HEAD_DIM).transpose(1, 0, 2)
    gdn = gated_delta_heads(p, conv_heads, gcol, grow, gdn_norm_w.reshape(1, GDN_HEAD_DIM),
                            n_ctx_chunks=n_ctx // CHUNK)
    qa, kta, va = attention_inputs(p, cos, sin, q_norm_w, k_norm_w)
    qa = qa.reshape(B, ATT_KV_HEADS, ATT_GROUP, Lt, ATT_HEAD_DIM)
    att = attention(qa, kta, va, n_ctx=n_ctx, skip_ctx_queries=skip_ctx_queries)
    if skip_ctx_queries:
        gdn = gdn[:, n_ctx:]
    mix = jnp.concatenate([gdn, att], axis=-1).astype(BF16)
    return matmul(mix.reshape(-1, mix.shape[-1]), w_out_bf, tm=MM_ROWS, tn=1024)


def _moe(h, layer, router_w_bf, router_bias, w_gate, w_up, w_down, sh_gate, sh_up, sh_down):
    T = h.shape[0]
    E = N_EXPERTS
    tm = EXPERT_ROWS
    eidx, wts, rank, counts = moe_router(h, router_w_bf, router_bias)
    counts = counts[0]
    padded = (counts + tm - 1) // tm * tm
    pad_ends = jnp.cumsum(padded)
    pad_starts = pad_ends - padded
    n_blocks = -(-T * TOP_K // tm) + E
    n_used = (pad_ends[-1] // tm).astype(jnp.int32)
    blk_first_row = jnp.minimum(jnp.arange(n_blocks, dtype=jnp.int32), n_used - 1) * tm
    blk_e = jnp.sum((pad_ends[None, :] <= blk_first_row[:, None]).astype(jnp.int32), axis=1)
    blk_e = jnp.minimum(blk_e, E - 1)
    eidx_flat, rank_flat = eidx.reshape(-1), rank.reshape(-1)
    xs, dest_flat = moe_dispatch(h, eidx_flat, rank_flat, pad_starts, pad_starts + counts, pad_ends, n_blocks * tm)
    out_rows = expert_mlp(xs, blk_e, n_used.reshape(1), layer, w_gate, w_up, w_down)
    n_sh = T // tm
    shared = expert_mlp(h, jnp.zeros((n_sh,), jnp.int32), jnp.full((1,), n_sh, jnp.int32), layer,
                        sh_gate[:, None], sh_up[:, None], sh_down[:, None])
    return moe_combine(out_rows, dest_flat, wts.reshape(-1), shared)


def kernel(x, c, ctx, c_ctx, ada_w, ada_b, w_in, conv_w, gdn_a_log, gdn_dt_bias, gdn_norm_w, q_norm_w, k_norm_w,
           w_out, ln1_w, ln1_b, router_w, router_bias, exp_w_gate, exp_w_up, exp_w_down, sh_w_gate, sh_w_up,
           sh_w_down, ln2_w, ln2_b):
    B, L, D = x.shape
    Lc = ctx.shape[1]
    Lt = Lc + L
    depth = ada_w.shape[0]
    alpha = (2.0 * depth) ** 0.25
    cos, sin = _rope_tables(Lc, L)
    xa = _standardize(jnp.concatenate([ctx, x], axis=1), LN_EPS)
    is_ctx = (jnp.arange(Lt) < Lc)[None, :, None]
    cond = jnp.concatenate([c, c_ctx[None], jnp.zeros((16 - B - 1, D), F32)], axis=0)
    cond = jax.nn.silu(cond)
    SH1, SC1, G1, SH2, SC2, G2 = range(6)
    mods = [(matmul(cond, ada_w[l], tm=16, tn=1024) + ada_b[l]).reshape(16, 1, 6 * D) for l in range(depth)]
    pick = lambda m, i: jnp.where(is_ctx, m[B, :, i * D:(i + 1) * D][None], m[:B, :, i * D:(i + 1) * D])
    h = (xa * (1.0 + pick(mods[0], SC1)) + pick(mods[0], SH1)).astype(BF16)
    for l in range(depth):
        last = l == depth - 1
        w_l = w_in[l]
        w_main = jnp.concatenate([w_l[:, :OFF_BA], w_l[:, OFF_ATT:]], axis=1).astype(BF16)
        w_ba = jnp.pad(w_l[:, OFF_BA:OFF_ATT], ((0, 0), (0, V7X_LANES - 4 * GDN_HEADS))).astype(BF16)
        y = _mixer(h.reshape(B * Lt, D), w_main, w_ba, conv_w[l], gdn_a_log[l], gdn_dt_bias[l], gdn_norm_w[l],
                   q_norm_w[l], k_norm_w[l], w_out[l].astype(BF16), cos, sin, B=B, Lt=Lt, n_ctx=Lc,
                   skip_ctx_queries=last)
        if last:
            xa = xa[:, Lc:]
        rows = xa.shape[1]
        n_ctx = 0 if last else Lc
        xa, h2 = residual_norm(xa, y, mods[l], G1, ln1_w[l], ln1_b[l], alpha=alpha, n_ctx=n_ctx, y_tiles=False,
                               h_mode="tiles", h_mod=mods[l], sc_chunk=SC2, sh_chunk=SH2)
        ff = _moe(h2.reshape(B * rows, V7X_SUBLANES, V7X_LANES), l, router_w[l].astype(BF16), router_bias[l],
                  exp_w_gate, exp_w_up, exp_w_down, sh_w_gate, sh_w_up, sh_w_down)
        ff = ff.reshape(B * rows * V7X_SUBLANES, V7X_LANES)
        if last:
            (xa,) = residual_norm(xa, ff, mods[l], G2, ln2_w[l], ln2_b[l], alpha=alpha, n_ctx=n_ctx, y_tiles=True)
        else:
            xa, h = residual_norm(xa, ff, mods[l], G2, ln2_w[l], ln2_b[l], alpha=alpha, n_ctx=n_ctx, y_tiles=True,
                                  h_mode="bf16", h_mod=mods[l + 1], sc_chunk=SC1, sh_chunk=SH1)
    return xa
```
